```python
import math
import jax
import jax.numpy as jnp
from jax import lax
import numpy as np

D_MODEL = 2048
BATCH = 2
SEQ = 16384
DEPTH = 4
DEC_BATCH = 1
DEC_SEQ = 8192
PAST_LEN = 128

N_MIXERS = 4
GRID_W = 64
RMS_EPS = 1e-6
D_FF = 5632
N_NORMS = 6

A_HEADS = 8
A_HEAD_DIM = D_MODEL // (2 * A_HEADS)
A_Q_BLOCK = 128

POOL_WINDOWS = (2, 4, 8, 16)
POOL_GROUP = D_MODEL // len(POOL_WINDOWS)

GDN_K_HEADS = 16
GDN_V_HEADS = 32
GDN_K_DIM = 128
GDN_V_DIM = 128
GDN_CONV = 4
GDN_CHUNK = 64
GDN_QK_W = GDN_K_HEADS * GDN_K_DIM
GDN_V_W = GDN_V_HEADS * GDN_V_DIM
GDN_CONV_DIM = 2 * GDN_QK_W + GDN_V_W
GDN_IN_W = GDN_CONV_DIM + GDN_V_W + 4 * GDN_V_HEADS

NA_HEADS = 16
NA_HEAD_DIM = D_MODEL // NA_HEADS
NA_WIN_ROWS = 8
NA_WIN_COLS = 16

N_A_LAYERS = (DEPTH + N_MIXERS - 1) // N_MIXERS
N_B_LAYERS = (DEPTH + N_MIXERS - 2) // N_MIXERS
N_C_LAYERS = (DEPTH + N_MIXERS - 3) // N_MIXERS
N_D_LAYERS = (DEPTH + N_MIXERS - 4) // N_MIXERS

kernel_name = "hybrid_bidir_encoder_trunk"


def rms_norm(x, g):
    xf = x.astype(jnp.float32)
    y = xf * lax.rsqrt(jnp.mean(xf * xf, axis=-1, keepdims=True) + RMS_EPS)
    return (y * g.astype(jnp.float32)).astype(x.dtype)


def swiglu_ffn(x, w_in, w_out):
    gate, up = jnp.split(x @ w_in, 2, axis=-1)
    return (jax.nn.silu(gate) * up) @ w_out


def lambda_init(layer_idx):
    return 0.8 - 0.6 * math.exp(-0.3 * layer_idx)


def alibi_slopes(n):
    return 2.0 ** (-8.0 * jnp.arange(1, n + 1, dtype=jnp.float32) / n)


def diff_attention(x, w_qkv, lam, subln, w_out, lam_init):
    b, s, _ = x.shape
    h, dh = A_HEADS, A_HEAD_DIM
    q, k, v = jnp.split(x @ w_qkv, 3, axis=-1)
    q = q.reshape(b, s, h, 2, dh).transpose(0, 2, 3, 1, 4)
    k = k.reshape(b, s, h, 2, dh).transpose(0, 2, 3, 1, 4)
    v = v.reshape(b, s, h, 2 * dh).transpose(0, 2, 1, 3)
    lf = lam.astype(jnp.float32)
    lam_full = jnp.exp(jnp.sum(lf[0] * lf[1])) - jnp.exp(jnp.sum(lf[2] * lf[3])) + lam_init
    slopes = alibi_slopes(h)
    pos = jnp.arange(s)
    nb = s // A_Q_BLOCK
    qb = q.reshape(b, h, 2, nb, A_Q_BLOCK, dh).transpose(3, 0, 1, 2, 4, 5)
    scale = dh ** -0.5

    def block(args):
        q_blk, blk = args
        t = blk * A_Q_BLOCK + jnp.arange(A_Q_BLOCK)
        dist = jnp.abs(t[:, None] - pos[None, :]).astype(jnp.float32)
        sc = jnp.einsum('bhmqd,bhmkd->bhmqk', q_blk, k, preferred_element_type=jnp.float32) * scale
        sc = sc - slopes[None, :, None, None, None] * dist[None, None, None]
        p = jax.nn.softmax(sc, axis=-1)
        attn = p[:, :, 0] - lam_full * p[:, :, 1]
        return jnp.einsum('bhqk,bhkd->bhqd', attn.astype(v.dtype), v)

    o = lax.map(block, (qb, jnp.arange(nb)))
    o = o.transpose(1, 2, 0, 3, 4).reshape(b, h, s, 2 * dh)
    o = rms_norm(o, subln) * (1.0 - lam_init)
    o = o.transpose(0, 2, 1, 3).reshape(b, s, h * 2 * dh)
    return o @ w_out


def pool_mixer(x, w_groups, scale):
    b, s, d = x.shape
    xf = x.astype(jnp.float32)
    cs = jnp.pad(jnp.cumsum(xf, axis=1), ((0, 0), (1, 0), (0, 0)))
    t = np.arange(s)
    outs = []
    for gi, w in enumerate(POOL_WINDOWS):
        c0, c1 = gi * POOL_GROUP, (gi + 1) * POOL_GROUP
        lo = np.clip(t - w // 2, 0, s)
        hi = np.clip(t + w // 2, 0, s)
        count = jnp.asarray((hi - lo).astype(np.float32))[None, :, None]
        window_sum = cs[:, hi, c0:c1] - cs[:, lo, c0:c1]
        pooled = window_sum / count - xf[:, :, c0:c1]
        outs.append(pooled.astype(x.dtype) @ w_groups[gi])
    return jnp.concatenate(outs, axis=-1) * scale


def l2_normalize(t):
    return t * lax.rsqrt(jnp.sum(t * t, axis=-1, keepdims=True) + RMS_EPS)


def gated_delta_chunked(q, k, v, g, beta):
    b, h, s, dk = q.shape
    dv = v.shape[-1]
    c = GDN_CHUNK
    n = s // c
    q = q.reshape(b, h, n, c, dk)
    k = k.reshape(b, h, n, c, dk)
    v = v.reshape(b, h, n, c, dv)
    beta = beta.reshape(b, h, n, c)
    g = jnp.cumsum(g.reshape(b, h, n, c), axis=-1)
    k_beta = k * beta[..., None]
    v_beta = v * beta[..., None]
    tri_strict = jnp.tril(jnp.ones((c, c), dtype=bool), -1)
    tri_incl = jnp.tril(jnp.ones((c, c), dtype=bool))
    decay = jnp.exp(jnp.where(tri_incl, g[..., :, None] - g[..., None, :], -jnp.inf))
    l_mat = jnp.where(tri_strict, jnp.einsum('bhnid,bhnjd->bhnij', k_beta, k) * decay, 0.0)
    rhs = jnp.concatenate([v_beta, k_beta * jnp.exp(g)[..., None]], axis=-1)
    sol = lax.linalg.triangular_solve(l_mat, rhs, left_side=True, lower=True, unit_diagonal=True)
    u, w = sol[..., :dv], sol[..., dv:]
    qk = jnp.where(tri_incl, jnp.einsum('bhnid,bhnjd->bhnij', q, k) * decay, 0.0)

    def step(state, inp):
        q_i, k_i, u_i, w_i, g_i, qk_i = inp
        v_new = u_i - jnp.einsum('bhck,bhkv->bhcv', w_i, state)
        o_i = jnp.einsum('bhck,bhkv->bhcv', q_i * jnp.exp(g_i)[..., None], state) + jnp.einsum('bhij,bhjv->bhiv', qk_i, v_new)
        g_last = g_i[..., -1]
        state = state * jnp.exp(g_last)[..., None, None] + jnp.einsum('bhck,bhcv->bhkv', k_i * jnp.exp(g_last[..., None] - g_i)[..., None], v_new)
        return state, o_i

    xs = (jnp.moveaxis(q, 2, 0), jnp.moveaxis(k, 2, 0), jnp.moveaxis(u, 2, 0), jnp.moveaxis(w, 2, 0), jnp.moveaxis(g, 2, 0), jnp.moveaxis(qk, 2, 0))
    s0 = jnp.zeros((b, h, dk, dv), jnp.float32)
    _, o = lax.scan(step, s0, xs)
    return jnp.moveaxis(o, 0, 2).reshape(b, h, s, dv)


def gdn_mixer(x, w_in, conv_w, a_log, dt_bias, norm_g, w_out):
    b, s, _ = x.shape
    qkv, z, ab = jnp.split(x @ w_in, [GDN_CONV_DIM, GDN_CONV_DIM + GDN_V_W], axis=-1)
    left = GDN_CONV // 2
    qkv = lax.conv_general_dilated(qkv, conv_w[:, None, :], window_strides=(1,), padding=[(left, GDN_CONV - 1 - left)], dimension_numbers=('NWC', 'WIO', 'NWC'), feature_group_count=GDN_CONV_DIM)
    qkv = jax.nn.silu(qkv).astype(jnp.float32)
    q, k, v = jnp.split(qkv, [GDN_QK_W, 2 * GDN_QK_W], axis=-1)
    rep = GDN_V_HEADS // GDN_K_HEADS
    q = l2_normalize(q.reshape(b, s, GDN_K_HEADS, GDN_K_DIM)) * (GDN_K_DIM ** -0.5)
    k = l2_normalize(k.reshape(b, s, GDN_K_HEADS, GDN_K_DIM))
    q = jnp.repeat(q, rep, axis=2).transpose(0, 2, 1, 3)
    k = jnp.repeat(k, rep, axis=2).transpose(0, 2, 1, 3)
    v = v.reshape(b, s, GDN_V_HEADS, GDN_V_DIM).transpose(0, 2, 1, 3)
    ab = ab.astype(jnp.float32).reshape(b, s, 2, 2, GDN_V_HEADS)
    g = -jnp.exp(a_log.astype(jnp.float32)) * jax.nn.softplus(ab[:, :, 0] + dt_bias.astype(jnp.float32))
    beta = jax.nn.sigmoid(ab[:, :, 1])
    g = g.transpose(2, 0, 3, 1)
    beta = beta.transpose(2, 0, 3, 1)
    o_fwd = gated_delta_chunked(q, k, v, g[0], beta[0])
    rev = lambda t: jnp.flip(t, axis=2)
    o_bwd = rev(gated_delta_chunked(rev(q), rev(k), rev(v), rev(g[1]), rev(beta[1])))
    o = (o_fwd + o_bwd).transpose(0, 2, 1, 3)
    o = rms_norm(o, norm_g) * jax.nn.silu(z.astype(jnp.float32).reshape(b, s, GDN_V_HEADS, GDN_V_DIM))
    return o.reshape(b, s, GDN_V_W).astype(x.dtype) @ w_out


def neighbourhood_attention(x, w_qkv, rpb, w_out):
    b, s, _ = x.shape
    rows = s // GRID_W
    kh = min(NA_WIN_ROWS, rows)
    kw = NA_WIN_COLS
    h, dh = NA_HEADS, NA_HEAD_DIM
    qkv = (x @ w_qkv).reshape(b, rows, GRID_W, 3, h, dh)
    q, k, v = qkv[:, :, :, 0], qkv[:, :, :, 1], qkv[:, :, :, 2]
    cols = np.arange(GRID_W)
    col_start = np.clip(cols - kw // 2, 0, GRID_W - kw)
    col_idx = col_start[:, None] + np.arange(kw)[None, :]
    dc = col_idx - cols[:, None]
    bias_cols = rpb[:, :, dc + NA_WIN_COLS - 1]
    scale = dh ** -0.5

    def row_block(r):
        rs = jnp.clip(r - kh // 2, 0, rows - kh)
        q_r = lax.dynamic_index_in_dim(q, r, axis=1, keepdims=False)
        k_win = lax.dynamic_slice_in_dim(k, rs, kh, axis=1)[:, :, col_idx]
        v_win = lax.dynamic_slice_in_dim(v, rs, kh, axis=1)[:, :, col_idx]
        sc = jnp.einsum('bqhd,biqjhd->bhqij', q_r, k_win, preferred_element_type=jnp.float32) * scale
        dr = rs + jnp.arange(kh) - r + NA_WIN_ROWS - 1
        bias = jnp.take(bias_cols, dr, axis=1).transpose(0, 2, 1, 3)
        sc = sc + bias[None].astype(jnp.float32)
        p = jax.nn.softmax(sc.reshape(b, h, GRID_W, kh * kw), axis=-1).reshape(sc.shape)
        return jnp.einsum('bhqij,biqjhd->bqhd', p.astype(v.dtype), v_win)

    o = lax.map(row_block, jnp.arange(rows))
    o = o.transpose(1, 0, 2, 3, 4).reshape(b, s, h * dh)
    return o @ w_out


def trunk(x, norms, ffn_w_in, ffn_w_out, a_w_qkv, a_lambda, a_subln, a_w_out, pool_w, pool_scale, gdn_w_in, gdn_conv, gdn_a_log, gdn_dt_bias, gdn_norm, gdn_w_out, na_w_qkv, na_rpb, na_w_out):
    for i in range(DEPTH):
        g = norms[i]
        x = x + 0.5 * rms_norm(swiglu_ffn(rms_norm(x, g[0]), ffn_w_in[i, 0], ffn_w_out[i, 0]), g[1])
        u = rms_norm(x, g[2])
        kind, j = i % N_MIXERS, i // N_MIXERS
        if kind == 0:
            m = diff_attention(u, a_w_qkv[j], a_lambda[j], a_subln[j], a_w_out[j], lambda_init(i))
        elif kind == 1:
            m = pool_mixer(u, pool_w[j], pool_scale[j])
        elif kind == 2:
            m = gdn_mixer(u, gdn_w_in[j], gdn_conv[j], gdn_a_log[j], gdn_dt_bias[j], gdn_norm[j], gdn_w_out[j])
        else:
            m = neighbourhood_attention(u, na_w_qkv[j], na_rpb[j], na_w_out[j])
        x = x + rms_norm(m, g[3])
        x = x + 0.5 * rms_norm(swiglu_ffn(rms_norm(x, g[4]), ffn_w_in[i, 1], ffn_w_out[i, 1]), g[5])
    return x


def setup_inputs(seed: int = 0) -> dict:
    key = jax.random.key(seed)
    ks = jax.random.split(key, 20)
    f32 = jnp.float32

    def normal(k, shape, scale):
        return jax.random.normal(k, shape, f32) * scale

    dt = jnp.exp(jax.random.uniform(ks[14], (N_C_LAYERS, 2, GDN_V_HEADS), f32, minval=math.log(1e-3), maxval=math.log(1e-1)))
    return {
        'x_prompt': normal(ks[0], (BATCH, SEQ, D_MODEL), 1.0),
        'x_sample': normal(ks[1], (DEC_BATCH, DEC_SEQ, D_MODEL), 1.0),
        'norms': 1.0 + normal(ks[2], (DEPTH, N_NORMS, D_MODEL), 0.05),
        'ffn_w_in': normal(ks[3], (DEPTH, 2, D_MODEL, 2 * D_FF), D_MODEL ** -0.5),
        'ffn_w_out': normal(ks[4], (DEPTH, 2, D_FF, D_MODEL), D_FF ** -0.5),
        'a_w_qkv': normal(ks[5], (N_A_LAYERS, D_MODEL, 3 * D_MODEL), D_MODEL ** -0.5),
        'a_lambda': normal(ks[6], (N_A_LAYERS, 4, A_HEAD_DIM), 0.1),
        'a_subln': 1.0 + normal(ks[7], (N_A_LAYERS, 2 * A_HEAD_DIM), 0.05),
        'a_w_out': normal(ks[8], (N_A_LAYERS, D_MODEL, D_MODEL), D_MODEL ** -0.5),
        'pool_w': normal(ks[9], (N_B_LAYERS, len(POOL_WINDOWS), POOL_GROUP, POOL_GROUP), POOL_GROUP ** -0.5),
        'pool_scale': 1.0 + normal(ks[10], (N_B_LAYERS, D_MODEL), 0.1),
        'gdn_w_in': normal(ks[11], (N_C_LAYERS, D_MODEL, GDN_IN_W), D_MODEL ** -0.5),
        'gdn_conv': normal(ks[12], (N_C_LAYERS, GDN_CONV, GDN_CONV_DIM), GDN_CONV ** -0.5),
        'gdn_a_log': jnp.log(jax.random.uniform(ks[13], (N_C_LAYERS, 2, GDN_V_HEADS), f32, minval=1.0, maxval=16.0)),
        'gdn_dt_bias': dt + jnp.log(-jnp.expm1(-dt)),
        'gdn_norm': 1.0 + normal(ks[15], (N_C_LAYERS, GDN_V_DIM), 0.05),
        'gdn_w_out': normal(ks[16], (N_C_LAYERS, GDN_V_W, D_MODEL), GDN_V_W ** -0.5),
        'na_w_qkv': normal(ks[17], (N_D_LAYERS, D_MODEL, 3 * D_MODEL), D_MODEL ** -0.5),
        'na_rpb': normal(ks[18], (N_D_LAYERS, NA_HEADS, 2 * NA_WIN_ROWS - 1, 2 * NA_WIN_COLS - 1), 0.5),
        'na_w_out': normal(ks[19], (N_D_LAYERS, D_MODEL, D_MODEL), D_MODEL ** -0.5),
    }


def reference(x_prompt, x_sample, norms, ffn_w_in, ffn_w_out, a_w_qkv, a_lambda, a_subln, a_w_out, pool_w, pool_scale, gdn_w_in, gdn_conv, gdn_a_log, gdn_dt_bias, gdn_norm, gdn_w_out, na_w_qkv, na_rpb, na_w_out):
    y_prompt = trunk(x_prompt, norms, ffn_w_in, ffn_w_out, a_w_qkv, a_lambda, a_subln, a_w_out, pool_w, pool_scale, gdn_w_in, gdn_conv, gdn_a_log, gdn_dt_bias, gdn_norm, gdn_w_out, na_w_qkv, na_rpb, na_w_out)
    y_sample = trunk(x_sample, norms, ffn_w_in, ffn_w_out, a_w_qkv, a_lambda, a_subln, a_w_out, pool_w, pool_scale, gdn_w_in, gdn_conv, gdn_a_log, gdn_dt_bias, gdn_norm, gdn_w_out, na_w_qkv, na_rpb, na_w_out)
    return (y_prompt, y_sample)
```

```python
import functools
import math

import jax
import jax.numpy as jnp
from jax import lax
from jax.experimental import pallas as pl
from jax.experimental.pallas import tpu as pltpu

F32 = jnp.float32
BF16 = jnp.bfloat16

RMS_EPS = 1e-6
N_MIXERS = 4
GRID_W = 64
A_Q_HEAD_DIM = 128
POOL_WINDOWS = (2, 4, 8, 16)
GDN_K_DIM = 128
GDN_V_DIM = 128
GDN_CONV = 4
GDN_CHUNK = 64
NA_HEAD_DIM = 128
NA_WIN_ROWS = 8
NA_WIN_COLS = 16
NA_ROW_BLOCK = 8
NEG_BIG = -1e30

VMEM_LIMIT_BYTES = 56 * 1024 * 1024


def _cparams(*sem):
    return pltpu.CompilerParams(dimension_semantics=sem, vmem_limit_bytes=VMEM_LIMIT_BYTES)


def _rms(x, g):
    ms = jnp.mean(x * x, axis=-1, keepdims=True)
    return x * lax.rsqrt(ms + RMS_EPS) * g


def _silu(x):
    return x * jax.nn.sigmoid(x)


def _dot(a, b):
    return jnp.dot(a, b, preferred_element_type=F32)


def _dot_nt(a, b):
    return lax.dot_general(a, b, (((1,), (1,)), ((), ())), preferred_element_type=F32)


def _dot_tn(a, b):
    return lax.dot_general(a, b, (((0,), (0,)), ((), ())), preferred_element_type=F32)


def _split3(x):
    hi = x.astype(BF16)
    r1 = x - hi.astype(F32)
    mid = r1.astype(BF16)
    lo = (r1 - mid.astype(F32)).astype(BF16)
    return hi, mid, lo


def _dot_exact_lhs(a_exact, x):
    hi, mid, lo = _split3(x)
    a = a_exact.astype(BF16)
    return _dot(a, hi) + _dot(a, mid) + _dot(a, lo)


def _dot_exact_rhs(x, b_exact):
    hi, mid, lo = _split3(x)
    b = b_exact.astype(BF16)
    return _dot(hi, b) + _dot(mid, b) + _dot(lo, b)


def _dot_hi(a, b):
    ah = a.astype(BF16)
    al = (a - ah.astype(F32)).astype(BF16)
    bh = b.astype(BF16)
    bl = (b - bh.astype(F32)).astype(BF16)
    return _dot(ah, bh) + _dot(ah, bl) + _dot(al, bh)


def _div_pow2(x, n):
    assert n & (n - 1) == 0
    return lax.shift_right_logical(x, int(math.log2(n)))


def _pick(n, prefs):
    for p in prefs:
        if n % p == 0:
            return p
    return n


def _ffn_kernel(x_ref, g0_ref, g1_ref, wg_ref, wu_ref, wo_ref, o_ref, xn_ref):
    j = pl.program_id(1)

    @pl.when(j == 0)
    def _():
        xn_ref[...] = _rms(x_ref[...], g0_ref[...]).astype(BF16)
        o_ref[...] = jnp.zeros_like(o_ref)

    xn = xn_ref[...]
    h = _dot(xn, wg_ref[...])
    u = _dot(xn, wu_ref[...])
    a = (_silu(h) * u).astype(BF16)
    o_ref[...] += _dot(a, wo_ref[...])

    @pl.when(j == pl.num_programs(1) - 1)
    def _():
        o_ref[...] = x_ref[...] + 0.5 * _rms(o_ref[...], g1_ref[...])


def ffn_half_step(x, g0, g1, w_in, w_out):
    t, d = x.shape
    f = w_out.shape[0]
    tm = _pick(t, (512, 256, 128, 64, 32, 16, 8))
    tf = _pick(f, (512, 256, 128))
    nf = f // tf
    return pl.pallas_call(
        _ffn_kernel,
        grid=(t // tm, nf),
        in_specs=[
            pl.BlockSpec((tm, d), lambda i, j: (i, 0)),
            pl.BlockSpec((1, d), lambda i, j: (0, 0)),
            pl.BlockSpec((1, d), lambda i, j: (0, 0)),
            pl.BlockSpec((d, tf), lambda i, j: (0, j)),
            pl.BlockSpec((d, tf), lambda i, j: (0, j + nf)),
            pl.BlockSpec((tf, d), lambda i, j: (j, 0)),
        ],
        out_specs=pl.BlockSpec((tm, d), lambda i, j: (i, 0)),
        out_shape=jax.ShapeDtypeStruct((t, d), F32),
        scratch_shapes=[pltpu.VMEM((tm, d), BF16)],
        compiler_params=_cparams("parallel", "arbitrary"),
        name="ffn_half_step",
    )(x, g0.reshape(1, d), g1.reshape(1, d), w_in, w_in, w_out)


def _norm_matmul_kernel(x_ref, g_ref, w_ref, o_ref, xn_ref):
    @pl.when(pl.program_id(1) == 0)
    def _():
        xn_ref[...] = _rms(x_ref[...], g_ref[...]).astype(BF16)

    o_ref[...] = _dot(xn_ref[...], w_ref[...]).astype(o_ref.dtype)


def norm_matmul(x, g, w, out_dtype):
    t, d = x.shape
    n = w.shape[1]
    tm = _pick(t, (512, 256, 128, 64, 32, 16, 8))
    tn = _pick(n, (1024, 512, 256, 128))
    return pl.pallas_call(
        _norm_matmul_kernel,
        grid=(t // tm, n // tn),
        in_specs=[
            pl.BlockSpec((tm, d), lambda i, j: (i, 0)),
            pl.BlockSpec((1, d), lambda i, j: (0, 0)),
            pl.BlockSpec((d, tn), lambda i, j: (0, j)),
        ],
        out_specs=pl.BlockSpec((tm, tn), lambda i, j: (i, j)),
        out_shape=jax.ShapeDtypeStruct((t, n), out_dtype),
        scratch_shapes=[pltpu.VMEM((tm, d), BF16)],
        compiler_params=_cparams("parallel", "arbitrary"),
        name="norm_matmul",
    )(x, g.reshape(1, d), w)


def _matmul_norm_res_kernel(y_ref, w_ref, g_ref, x_ref, o_ref):
    k = pl.program_id(1)

    @pl.when(k == 0)
    def _():
        o_ref[...] = jnp.zeros_like(o_ref)

    o_ref[...] += _dot(y_ref[...], w_ref[...])

    @pl.when(k == pl.num_programs(1) - 1)
    def _():
        o_ref[...] = x_ref[...] + _rms(o_ref[...], g_ref[...])


def matmul_norm_res(y, w, g, x):
    t, kdim = y.shape
    d = w.shape[1]
    tm = _pick(t, (512, 256, 128, 64, 32, 16, 8))
    tk = _pick(kdim, (1024, 512, 256, 128))
    return pl.pallas_call(
        _matmul_norm_res_kernel,
        grid=(t // tm, kdim // tk),
        in_specs=[
            pl.BlockSpec((tm, tk), lambda i, k: (i, k)),
            pl.BlockSpec((tk, d), lambda i, k: (k, 0)),
            pl.BlockSpec((1, d), lambda i, k: (0, 0)),
            pl.BlockSpec((tm, d), lambda i, k: (i, 0)),
        ],
        out_specs=pl.BlockSpec((tm, d), lambda i, k: (i, 0)),
        out_shape=jax.ShapeDtypeStruct((t, d), F32),
        compiler_params=_cparams("parallel", "arbitrary"),
        name="matmul_norm_res",
    )(y, w, g.reshape(1, d), x)


def _diff_attn_kernel(slopes_ref, lam_ref, subln_ref, q_ref, k_ref, v_ref, o_ref,
                      base_ref, m_ref, l_ref, acc_ref, *, lam_init, tq, tk):
    h = pl.program_id(1)
    qi = pl.program_id(2)
    ki = pl.program_id(3)
    dh = A_Q_HEAD_DIM

    @pl.when(ki == 0)
    def _():
        rows = lax.broadcasted_iota(jnp.int32, (tq, tk), 0)
        cols = lax.broadcasted_iota(jnp.int32, (tq, tk), 1)
        base_ref[...] = (rows - cols).astype(F32)
        m_ref[...] = jnp.full_like(m_ref, NEG_BIG)
        l_ref[...] = jnp.zeros_like(l_ref)
        acc_ref[...] = jnp.zeros_like(acc_ref)

    off = (qi * tq - ki * tk).astype(F32)
    bias = slopes_ref[h] * jnp.abs(base_ref[...] + off)
    v = v_ref[...]
    for m in range(2):
        q = q_ref[:, m * dh:(m + 1) * dh]
        k = k_ref[:, m * dh:(m + 1) * dh]
        s = _dot_nt(q, k) - bias
        m_old = m_ref[m]
        m_new = jnp.maximum(m_old, jnp.max(s, axis=-1, keepdims=True))
        alpha = jnp.exp(m_old - m_new)
        p = jnp.exp(s - m_new)
        l_ref[m] = alpha * l_ref[m] + jnp.sum(p, axis=-1, keepdims=True)
        acc_ref[m] = alpha * acc_ref[m] + _dot(p.astype(BF16), v)
        m_ref[m] = m_new

    @pl.when(ki == pl.num_programs(3) - 1)
    def _():
        lf = lam_ref[...]
        lam_full = (jnp.exp(jnp.sum(lf[0:1] * lf[1:2], axis=-1, keepdims=True))
                    - jnp.exp(jnp.sum(lf[2:3] * lf[3:4], axis=-1, keepdims=True)) + lam_init)
        o = acc_ref[0] / l_ref[0] - lam_full * (acc_ref[1] / l_ref[1])
        o_ref[...] = (_rms(o, subln_ref[...]) * (1.0 - lam_init)).astype(o_ref.dtype)


def diff_attention_core(qkv, lam, subln, lam_init):
    b, s, d3 = qkv.shape
    d = d3 // 3
    hw = 2 * A_Q_HEAD_DIM
    nh = d // hw
    tq = _pick(s, (1024, 512, 256, 128))
    tk = tq
    slopes = 2.0 ** (-8.0 * jnp.arange(1, nh + 1, dtype=F32) / nh)
    kern = functools.partial(_diff_attn_kernel, lam_init=lam_init, tq=tq, tk=tk)
    return pl.pallas_call(
        kern,
        grid=(b, nh, s // tq, s // tk),
        in_specs=[
            pl.BlockSpec(memory_space=pltpu.SMEM),
            pl.BlockSpec((4, A_Q_HEAD_DIM), lambda bi, h, qi, ki: (0, 0)),
            pl.BlockSpec((1, hw), lambda bi, h, qi, ki: (0, 0)),
            pl.BlockSpec((None, tq, hw), lambda bi, h, qi, ki: (bi, qi, h)),
            pl.BlockSpec((None, tk, hw), lambda bi, h, qi, ki: (bi, ki, nh + h)),
            pl.BlockSpec((None, tk, hw), lambda bi, h, qi, ki: (bi, ki, 2 * nh + h)),
        ],
        out_specs=pl.BlockSpec((None, tq, hw), lambda bi, h, qi, ki: (bi, qi, h)),
        out_shape=jax.ShapeDtypeStruct((b, s, d), BF16),
        scratch_shapes=[
            pltpu.VMEM((tq, tk), F32),
            pltpu.VMEM((2, tq, 1), F32),
            pltpu.VMEM((2, tq, 1), F32),
            pltpu.VMEM((2, tq, hw), F32),
        ],
        compiler_params=_cparams("parallel", "parallel", "parallel", "arbitrary"),
        name="diff_attention_core",
    )(slopes, lam.astype(F32), subln.reshape(1, hw).astype(F32), qkv, qkv, qkv)


POOL_HALO = 16


def _pool_kernel(xp_ref, xc_ref, xn_ref, gin_ref, gout_ref, w_ref, scale_ref, o_ref, *, seq, tm):
    i = pl.program_id(1)
    n = pl.num_programs(1)
    gin = gin_ref[...]
    xc = xc_ref[...]
    uc = _rms(xc, gin)
    up = _rms(xp_ref[...], gin) * (i > 0).astype(F32)
    un = _rms(xn_ref[...], gin) * (i < n - 1).astype(F32)
    grp = w_ref.shape[1]
    hal = POOL_HALO

    r_c = lax.broadcasted_iota(jnp.int32, (tm, tm), 0)
    c_c = lax.broadcasted_iota(jnp.int32, (tm, tm), 1)
    d_c = c_c - r_c
    r_h = lax.broadcasted_iota(jnp.int32, (tm, hal), 0)
    c_h = lax.broadcasted_iota(jnp.int32, (tm, hal), 1)
    d_p = c_h - hal - r_h
    d_n = c_h + tm - r_h
    t_abs = i * tm + lax.broadcasted_iota(jnp.int32, (tm, 1), 0)

    outs = []
    for gi, win in enumerate(POOL_WINDOWS):
        half = win // 2
        sl = slice(gi * grp, (gi + 1) * grp)
        band_c = ((d_c >= -half) & (d_c < half)).astype(F32)
        band_p = (d_p >= -half).astype(F32)
        band_n = (d_n < half).astype(F32)
        wsum = (_dot_exact_lhs(band_c, uc[:, sl]) + _dot_exact_lhs(band_p, up[:, sl])
                + _dot_exact_lhs(band_n, un[:, sl]))
        lo = jnp.clip(t_abs - half, 0, seq)
        hi = jnp.clip(t_abs + half, 0, seq)
        count = (hi - lo).astype(F32)
        pooled = wsum / count - uc[:, sl]
        outs.append(_dot(pooled.astype(BF16), w_ref[gi]))
    y = jnp.concatenate(outs, axis=-1) * scale_ref[...]
    o_ref[...] = xc + _rms(y, gout_ref[...])


def pool_sublayer(x, g_in, g_out, w_groups, scale):
    b, s, d = x.shape
    tm = _pick(s, (256, 128, 64, 32, 16))
    hal = POOL_HALO
    nh = tm // hal
    last = s // hal - 1
    kern = functools.partial(_pool_kernel, seq=s, tm=tm)
    grp = w_groups.shape[1]
    return pl.pallas_call(
        kern,
        grid=(b, s // tm),
        in_specs=[
            pl.BlockSpec((None, hal, d), lambda bi, i: (bi, jnp.maximum(i * nh - 1, 0), 0)),
            pl.BlockSpec((None, tm, d), lambda bi, i: (bi, i, 0)),
            pl.BlockSpec((None, hal, d), lambda bi, i: (bi, jnp.minimum((i + 1) * nh, last), 0)),
            pl.BlockSpec((1, d), lambda bi, i: (0, 0)),
            pl.BlockSpec((1, d), lambda bi, i: (0, 0)),
            pl.BlockSpec((len(POOL_WINDOWS), grp, grp), lambda bi, i: (0, 0, 0)),
            pl.BlockSpec((1, d), lambda bi, i: (0, 0)),
        ],
        out_specs=pl.BlockSpec((None, tm, d), lambda bi, i: (bi, i, 0)),
        out_shape=jax.ShapeDtypeStruct((b, s, d), F32),
        compiler_params=_cparams("parallel", "parallel"),
        name="pool_sublayer",
    )(x, x, x, g_in.reshape(1, d), g_out.reshape(1, d), w_groups, scale.reshape(1, d).astype(F32))


CONV_HALO = 16


def _gdn_conv_kernel(xp_ref, xc_ref, xn_ref, w_ref, o_ref, ext_ref, *, tb, n_q_blocks, n_qk_blocks):
    i = pl.program_id(1)
    j = pl.program_id(2)
    n = pl.num_programs(1)
    hal = CONV_HALO
    left = GDN_CONV // 2
    ext_ref[0:hal, :] = xp_ref[...].astype(F32) * (i > 0).astype(F32)
    ext_ref[hal:hal + tb, :] = xc_ref[...].astype(F32)
    ext_ref[hal + tb:hal + tb + hal, :] = xn_ref[...].astype(F32) * (i < n - 1).astype(F32)
    acc = None
    for kk in range(GDN_CONV):
        start = hal - left + kk
        term = ext_ref[start:start + tb, :] * w_ref[kk:kk + 1, :]
        acc = term if acc is None else acc + term
    y = _silu(acc)

    @pl.when(j >= n_qk_blocks)
    def _():
        o_ref[...] = y.astype(o_ref.dtype)

    @pl.when(j < n_qk_blocks)
    def _():
        qscale = jnp.where(j < n_q_blocks, GDN_K_DIM ** -0.5, 1.0).astype(F32)
        for hh in range(y.shape[1] // GDN_K_DIM):
            sl = slice(hh * GDN_K_DIM, (hh + 1) * GDN_K_DIM)
            t = y[:, sl]
            nrm = lax.rsqrt(jnp.sum(t * t, axis=-1, keepdims=True) + RMS_EPS) * qscale
            o_ref[:, sl] = (t * nrm).astype(o_ref.dtype)


def gdn_conv(qkvz, conv_w, qk_w, conv_dim):
    b, s, _ = qkvz.shape
    tb = _pick(s, (256, 128, 64, 32, 16))
    cb = _pick(qk_w, (1024, 512, 256, 128))
    hal = CONV_HALO
    nh = tb // hal
    last = s // hal - 1
    kern = functools.partial(_gdn_conv_kernel, tb=tb, n_q_blocks=qk_w // cb, n_qk_blocks=2 * qk_w // cb)
    return pl.pallas_call(
        kern,
        grid=(b, s // tb, conv_dim // cb),
        in_specs=[
            pl.BlockSpec((None, hal, cb), lambda bi, i, j: (bi, jnp.maximum(i * nh - 1, 0), j)),
            pl.BlockSpec((None, tb, cb), lambda bi, i, j: (bi, i, j)),
            pl.BlockSpec((None, hal, cb), lambda bi, i, j: (bi, jnp.minimum((i + 1) * nh, last), j)),
            pl.BlockSpec((GDN_CONV, cb), lambda bi, i, j: (0, j)),
        ],
        out_specs=pl.BlockSpec((None, tb, cb), lambda bi, i, j: (bi, i, j)),
        out_shape=jax.ShapeDtypeStruct((b, s, conv_dim), BF16),
        scratch_shapes=[pltpu.VMEM((tb + 2 * hal, cb), F32)],
        compiler_params=_cparams("parallel", "parallel", "parallel"),
        name="gdn_conv",
    )(qkvz, qkvz, qkvz, conv_w.astype(F32))


def _gdn_core_kernel(q_ref, k_ref, v_ref, ab_ref, alog_ref, dt_ref, o_ref, s_ref, *,
                     direction, tb, n_vheads):
    kh = pl.program_id(1)
    blk = pl.program_id(2)
    c = GDN_CHUNK
    nchunk = tb // c
    dk = GDN_K_DIM
    dv = GDN_V_DIM
    hv2 = 2 * n_vheads

    @pl.when(blk == 0)
    def _():
        s_ref[...] = jnp.zeros_like(s_ref)

    ab = ab_ref[...]
    lane = lax.broadcasted_iota(jnp.int32, ab.shape, 1)
    g_all = -jnp.exp(alog_ref[...]) * jax.nn.softplus(ab + dt_ref[...])
    beta_all = jax.nn.sigmoid(ab)

    r_t = lax.broadcasted_iota(jnp.int32, (tb, tb), 0)
    c_t = lax.broadcasted_iota(jnp.int32, (tb, tb), 1)
    same_chunk = _div_pow2(r_t, c) == _div_pow2(c_t, c)
    tri = (c_t <= r_t) if direction == 0 else (c_t >= r_t)
    cum_mat = (same_chunk & tri).astype(F32)
    gc_all = _dot_exact_lhs(cum_mat, jnp.where(lane < hv2, g_all, 0.0))

    sel_r = lax.broadcasted_iota(jnp.int32, (ab.shape[1], 2 * dv), 0)
    sel_c = lax.broadcasted_iota(jnp.int32, (ab.shape[1], 2 * dv), 1)
    a_col = direction * n_vheads + 2 * kh + _div_pow2(sel_c, dv)
    sel_g = (sel_r == a_col).astype(F32)
    sel_b = (sel_r == a_col + hv2).astype(F32)
    gcb_all = _dot_exact_rhs(gc_all, sel_g)
    betab_all = _dot_exact_rhs(beta_all, sel_b)

    ri = lax.broadcasted_iota(jnp.int32, (c, c), 0)
    ci = lax.broadcasted_iota(jnp.int32, (c, c), 1)
    if direction == 0:
        m_incl, m_strict, last_row = ri >= ci, ri > ci, c - 1
    else:
        m_incl, m_strict, last_row = ri <= ci, ri < ci, 0
    eye = (ri == ci).astype(F32)

    order = range(nchunk) if direction == 0 else range(nchunk - 1, -1, -1)
    for ch in order:
        rows = slice(ch * c, (ch + 1) * c)
        q = q_ref[rows, :]
        k = k_ref[rows, :]
        kk = _dot_nt(k, k)
        qk = _dot_nt(q, k)
        kf = k.astype(F32)
        qf = q.astype(F32)
        for e in range(2):
            lanes = slice(e * dv, (e + 1) * dv)
            gcb = gcb_all[rows, lanes]
            bb = betab_all[rows, lanes]
            v = v_ref[rows, lanes].astype(F32)
            g_row = jnp.transpose(gcb)[:c, :]
            decay = jnp.exp(jnp.where(m_incl, gcb[:, :c] - g_row, NEG_BIG))
            a_mat = jnp.where(m_strict, bb[:, :c] * kk * decay, 0.0)
            mp = -a_mat
            t_inv = eye + mp
            for _ in range(int(math.log2(c)) - 1):
                mp = _dot_hi(mp, mp)
                t_inv = t_inv + _dot_hi(t_inv, mp)
            eg = jnp.exp(gcb)
            rhs = jnp.concatenate([v * bb, kf * (bb * eg)], axis=-1)
            sol = _dot(t_inv.astype(BF16), rhs.astype(BF16))
            u, w = sol[:, :dv], sol[:, dv:]
            qk_m = jnp.where(m_incl, qk * decay, 0.0)
            state = s_ref[e]
            state_b = state.astype(BF16)
            v_new = u - _dot(w.astype(BF16), state_b)
            o = _dot((qf * eg).astype(BF16), state_b) + _dot(qk_m.astype(BF16), v_new.astype(BF16))
            o_ref[rows, lanes] = o.astype(o_ref.dtype)
            g_last = gcb[last_row:last_row + 1, :]
            k_dec = kf * jnp.exp(g_last - gcb)
            s_ref[e] = state * jnp.exp(g_last) + _dot_tn(k_dec.astype(BF16), v_new.astype(BF16))


def gdn_core(qkv, ab, a_log, dt_bias, direction, n_kheads, n_vheads):
    b, s, _ = qkv.shape
    tb = _pick(s, (256, 128, 64))
    nblk = s // tb
    abw = ab.shape[-1]
    pad = jnp.zeros((2 * n_vheads,), F32)
    alog_row = jnp.concatenate([a_log.reshape(-1).astype(F32), pad]).reshape(1, abw)
    dt_row = jnp.concatenate([dt_bias.reshape(-1).astype(F32), pad]).reshape(1, abw)

    def tok(blk):
        return blk if direction == 0 else nblk - 1 - blk

    kern = functools.partial(_gdn_core_kernel, direction=direction, tb=tb, n_vheads=n_vheads)
    return pl.pallas_call(
        kern,
        grid=(b, n_kheads, nblk),
        in_specs=[
            pl.BlockSpec((None, tb, GDN_K_DIM), lambda bi, kh, blk: (bi, tok(blk), kh)),
            pl.BlockSpec((None, tb, GDN_K_DIM), lambda bi, kh, blk: (bi, tok(blk), n_kheads + kh)),
            pl.BlockSpec((None, tb, 2 * GDN_V_DIM), lambda bi, kh, blk: (bi, tok(blk), n_kheads + kh)),
            pl.BlockSpec((None, tb, abw), lambda bi, kh, blk: (bi, tok(blk), 0)),
            pl.BlockSpec((1, abw), lambda bi, kh, blk: (0, 0)),
            pl.BlockSpec((1, abw), lambda bi, kh, blk: (0, 0)),
        ],
        out_specs=pl.BlockSpec((None, tb, 2 * GDN_V_DIM), lambda bi, kh, blk: (bi, tok(blk), kh)),
        out_shape=jax.ShapeDtypeStruct((b, s, n_vheads * GDN_V_DIM), F32),
        scratch_shapes=[pltpu.VMEM((2, GDN_K_DIM, GDN_V_DIM), F32)],
        compiler_params=_cparams("parallel", "parallel", "arbitrary"),
        name="gdn_core_fwd" if direction == 0 else "gdn_core_bwd",
    )(qkv, qkv, qkv, ab, alog_row, dt_row)


def _gdn_gate_kernel(of_ref, ob_ref, z_ref, g_ref, o_ref):
    g = g_ref[...]
    for hh in range(of_ref.shape[1] // GDN_V_DIM):
        sl = slice(hh * GDN_V_DIM, (hh + 1) * GDN_V_DIM)
        o = of_ref[:, sl] + ob_ref[:, sl]
        z = z_ref[:, sl].astype(F32)
        o_ref[:, sl] = (_rms(o, g) * _silu(z)).astype(o_ref.dtype)


def gdn_gate(o_fwd, o_bwd, qkvz, norm_g, z_col0):
    t, w = o_fwd.shape
    tm = _pick(t, (512, 256, 128, 64, 32, 16))
    cb = _pick(w, (1024, 512, 256, 128))
    zoff = z_col0 // cb
    return pl.pallas_call(
        _gdn_gate_kernel,
        grid=(t // tm, w // cb),
        in_specs=[
            pl.BlockSpec((tm, cb), lambda i, j: (i, j)),
            pl.BlockSpec((tm, cb), lambda i, j: (i, j)),
            pl.BlockSpec((tm, cb), lambda i, j: (i, zoff + j)),
            pl.BlockSpec((1, GDN_V_DIM), lambda i, j: (0, 0)),
        ],
        out_specs=pl.BlockSpec((tm, cb), lambda i, j: (i, j)),
        out_shape=jax.ShapeDtypeStruct((t, w), BF16),
        compiler_params=_cparams("parallel", "parallel"),
        name="gdn_gate",
    )(o_fwd, o_bwd, qkvz, norm_g.reshape(1, GDN_V_DIM).astype(F32))


def _na_kernel(bias_ref, q_ref, kp_ref, kc_ref, kn_ref, vp_ref, vc_ref, vn_ref, o_ref, *, grid_rows):
    i = pl.program_id(2)
    rb = NA_ROW_BLOCK
    tq = rb * GRID_W
    q = q_ref[...]
    s = jnp.concatenate([_dot_nt(q, kp_ref[...]), _dot_nt(q, kc_ref[...]), _dot_nt(q, kn_ref[...])],
                        axis=-1)
    q_row = i * rb + _div_pow2(lax.broadcasted_iota(jnp.int32, (tq, 1), 0), GRID_W)
    win_lo = jnp.clip(q_row - NA_WIN_ROWS // 2, 0, grid_rows - NA_WIN_ROWS)
    k_row = (i - 1) * rb + _div_pow2(lax.broadcasted_iota(jnp.int32, (1, 3 * tq), 1), GRID_W)
    valid = (k_row >= win_lo) & (k_row < win_lo + NA_WIN_ROWS)
    s = jnp.where(valid, s + bias_ref[...], NEG_BIG)
    m = jnp.max(s, axis=-1, keepdims=True)
    p = jnp.exp(s - m)
    l = jnp.sum(p, axis=-1, keepdims=True)
    pb = p.astype(BF16)
    o = (_dot(pb[:, :tq], vp_ref[...]) + _dot(pb[:, tq:2 * tq], vc_ref[...])
         + _dot(pb[:, 2 * tq:], vn_ref[...]))
    o_ref[...] = (o / l).astype(o_ref.dtype)


def _na_dense_bias(rpb):
    h = rpb.shape[0]
    rb, w, kw, khw = NA_ROW_BLOCK, GRID_W, NA_WIN_COLS, NA_WIN_ROWS
    cols = jnp.arange(w)
    col_start = jnp.clip(cols - kw // 2, 0, w - kw)
    dc = cols[None, :] - cols[:, None]
    col_ok = (cols[None, :] >= col_start[:, None]) & (cols[None, :] < col_start[:, None] + kw)
    tiles = rpb[:, :, jnp.clip(dc + kw - 1, 0, 2 * kw - 2)]
    tiles = jnp.where(col_ok[None, None], tiles.astype(F32), NEG_BIG)
    tiles = jnp.concatenate([tiles, jnp.full((h, 1, w, w), NEG_BIG, F32)], axis=1)
    rq = jnp.arange(rb)[:, None]
    rk = jnp.arange(3 * rb)[None, :]
    dr = rk - rb - rq
    idx = jnp.where(jnp.abs(dr) <= khw - 1, dr + khw - 1, 2 * khw - 1)
    dense = tiles[:, idx]
    dense = dense.transpose(0, 1, 3, 2, 4).reshape(h, rb * w, 3 * rb * w)
    return dense


def na_core(qkv, rpb):
    b, s, d3 = qkv.shape
    d = d3 // 3
    dh = NA_HEAD_DIM
    nh = d // dh
    grid_rows = s // GRID_W
    tq = NA_ROW_BLOCK * GRID_W
    nblk = s // tq
    bias = _na_dense_bias(rpb)

    def prev(i):
        return jnp.maximum(i - 1, 0)

    def nxt(i):
        return jnp.minimum(i + 1, nblk - 1)

    kern = functools.partial(_na_kernel, grid_rows=grid_rows)
    return pl.pallas_call(
        kern,
        grid=(b, nh, nblk),
        in_specs=[
            pl.BlockSpec((None, tq, 3 * tq), lambda bi, h, i: (h, 0, 0)),
            pl.BlockSpec((None, tq, dh), lambda bi, h, i: (bi, i, h)),
            pl.BlockSpec((None, tq, dh), lambda bi, h, i: (bi, prev(i), nh + h)),
            pl.BlockSpec((None, tq, dh), lambda bi, h, i: (bi, i, nh + h)),
            pl.BlockSpec((None, tq, dh), lambda bi, h, i: (bi, nxt(i), nh + h)),
            pl.BlockSpec((None, tq, dh), lambda bi, h, i: (bi, prev(i), 2 * nh + h)),
            pl.BlockSpec((None, tq, dh), lambda bi, h, i: (bi, i, 2 * nh + h)),
            pl.BlockSpec((None, tq, dh), lambda bi, h, i: (bi, nxt(i), 2 * nh + h)),
        ],
        out_specs=pl.BlockSpec((None, tq, dh), lambda bi, h, i: (bi, i, h)),
        out_shape=jax.ShapeDtypeStruct((b, s, d), BF16),
        compiler_params=_cparams("parallel", "parallel", "parallel"),
        name="na_core",
    )(bias, qkv, qkv, qkv, qkv, qkv, qkv, qkv)


def _lambda_init(layer_idx):
    return 0.8 - 0.6 * math.exp(-0.3 * layer_idx)


def _scaled_q_weight(w_qkv, scale):
    d = w_qkv.shape[1] // 3
    return jnp.concatenate([w_qkv[:, :d] * scale, w_qkv[:, d:]], axis=1).astype(BF16)


def _trunk(x, p):
    b, s, d = x.shape
    depth = p['norms'].shape[0]
    xt = x.reshape(b * s, d)
    for i in range(depth):
        g = p['norms'][i]
        xt = ffn_half_step(xt, g[0], g[1], p['ffn_w_in'][i][0], p['ffn_w_out'][i][0])
        kind, j = i % N_MIXERS, i // N_MIXERS
        if kind == 0:
            w_qkv = _scaled_q_weight(p['a_w_qkv'][j], A_Q_HEAD_DIM ** -0.5)
            qkv = norm_matmul(xt, g[2], w_qkv, BF16).reshape(b, s, 3 * d)
            o = diff_attention_core(qkv, p['a_lambda'][j], p['a_subln'][j], _lambda_init(i))
            xt = matmul_norm_res(o.reshape(b * s, d), p['a_w_out'][j].astype(BF16), g[3], xt)
        elif kind == 1:
            xt = pool_sublayer(xt.reshape(b, s, d), g[2], g[3], p['pool_w'][j].astype(BF16),
                               p['pool_scale'][j]).reshape(b * s, d)
        elif kind == 2:
            n_vheads = p['gdn_a_log'].shape[-1]
            v_w = n_vheads * GDN_V_DIM
            w_in = p['gdn_w_in'][j]
            conv_dim = p['gdn_conv'].shape[-1]
            qk_w = (conv_dim - v_w) // 2
            n_kheads = qk_w // GDN_K_DIM
            qkvz = norm_matmul(xt, g[2], w_in[:, :conv_dim + v_w].astype(BF16), BF16)
            ab = norm_matmul(xt, g[2], w_in[:, conv_dim + v_w:].astype(BF16), F32)
            qkv = gdn_conv(qkvz.reshape(b, s, conv_dim + v_w), p['gdn_conv'][j], qk_w, conv_dim)
            ab3 = ab.reshape(b, s, 4 * n_vheads)
            o_f = gdn_core(qkv, ab3, p['gdn_a_log'][j], p['gdn_dt_bias'][j], 0, n_kheads, n_vheads)
            o_b = gdn_core(qkv, ab3, p['gdn_a_log'][j], p['gdn_dt_bias'][j], 1, n_kheads, n_vheads)
            gated = gdn_gate(o_f.reshape(b * s, v_w), o_b.reshape(b * s, v_w), qkvz,
                             p['gdn_norm'][j], conv_dim)
            xt = matmul_norm_res(gated, p['gdn_w_out'][j].astype(BF16), g[3], xt)
        else:
            w_qkv = _scaled_q_weight(p['na_w_qkv'][j], NA_HEAD_DIM ** -0.5)
            qkv = norm_matmul(xt, g[2], w_qkv, BF16).reshape(b, s, 3 * d)
            o = na_core(qkv, p['na_rpb'][j])
            xt = matmul_norm_res(o.reshape(b * s, d), p['na_w_out'][j].astype(BF16), g[3], xt)
        xt = ffn_half_step(xt, g[4], g[5], p['ffn_w_in'][i][1], p['ffn_w_out'][i][1])
    return xt.reshape(b, s, d)


def kernel(x_prompt, x_sample, norms, ffn_w_in, ffn_w_out, a_w_qkv, a_lambda, a_subln, a_w_out, pool_w, pool_scale, gdn_w_in, gdn_conv, gdn_a_log, gdn_dt_bias, gdn_norm, gdn_w_out, na_w_qkv, na_rpb, na_w_out):
    p = dict(norms=norms, ffn_w_in=ffn_w_in.astype(BF16), ffn_w_out=ffn_w_out.astype(BF16),
             a_w_qkv=a_w_qkv, a_lambda=a_lambda, a_subln=a_subln, a_w_out=a_w_out, pool_w=pool_w,
             pool_scale=pool_scale, gdn_w_in=gdn_w_in, gdn_conv=gdn_conv, gdn_a_log=gdn_a_log,
             gdn_dt_bias=gdn_dt_bias, gdn_norm=gdn_norm, gdn_w_out=gdn_w_out, na_w_qkv=na_w_qkv,
             na_rpb=na_rpb, na_w_out=na_w_out)
    return (_trunk(x_prompt, p), _trunk(x_sample, p))
```

```python
import functools
import math

import jax
import jax.numpy as jnp
from jax import lax
from jax.experimental import pallas as pl
from jax.experimental.pallas import tpu as pltpu

F32 = jnp.float32
BF16 = jnp.bfloat16

RMS_EPS = 1e-6
LOG2_E = math.log2(math.e)
N_MIXERS = 4
GRID_W = 64
A_Q_HEAD_DIM = 128
POOL_WINDOWS = (2, 4, 8, 16)
GDN_K_DIM = 128
GDN_V_DIM = 128
GDN_CONV = 4
GDN_CHUNK = 64
NA_HEAD_DIM = 128
NA_WIN_ROWS = 8
NA_WIN_COLS = 16
NA_ROW_BLOCK = 8
NEG_BIG = -1e30

VMEM_LIMIT_BYTES = 56 * 1024 * 1024


def _cparams(*sem):
    return pltpu.CompilerParams(dimension_semantics=sem, vmem_limit_bytes=VMEM_LIMIT_BYTES)


def _rms(x, g):
    ms = jnp.mean(x * x, axis=-1, keepdims=True)
    return x * lax.rsqrt(ms + RMS_EPS) * g


def _silu(x):
    return x * jax.nn.sigmoid(x)


def _dot(a, b):
    return jnp.dot(a, b, preferred_element_type=F32)


def _dot_nt(a, b):
    return lax.dot_general(a, b, (((1,), (1,)), ((), ())), preferred_element_type=F32)


def _dot_tn(a, b):
    return lax.dot_general(a, b, (((0,), (0,)), ((), ())), preferred_element_type=F32)


def _split3(x):
    hi = x.astype(BF16)
    r1 = x - hi.astype(F32)
    mid = r1.astype(BF16)
    lo = (r1 - mid.astype(F32)).astype(BF16)
    return hi, mid, lo


def _dot_exact_lhs(a_exact, x):
    hi, mid, lo = _split3(x)
    a = a_exact.astype(BF16)
    return _dot(a, hi) + _dot(a, mid) + _dot(a, lo)


def _dot_exact_rhs(x, b_exact):
    hi, mid, lo = _split3(x)
    b = b_exact.astype(BF16)
    return _dot(hi, b) + _dot(mid, b) + _dot(lo, b)


def _dot_hi(a, b):
    ah = a.astype(BF16)
    al = (a - ah.astype(F32)).astype(BF16)
    bh = b.astype(BF16)
    bl = (b - bh.astype(F32)).astype(BF16)
    return _dot(ah, bh) + _dot(ah, bl) + _dot(al, bh)


def _div_pow2(x, n):
    assert n & (n - 1) == 0
    return lax.shift_right_logical(x, int(math.log2(n)))


def _pick(n, prefs):
    for p in prefs:
        if n % p == 0:
            return p
    return n


def _ffn_kernel(x_ref, g0_ref, g1_ref, wg_ref, wu_ref, wo_ref, o_ref, xn_ref):
    j = pl.program_id(1)

    @pl.when(j == 0)
    def _():
        xn_ref[...] = _rms(x_ref[...], g0_ref[...]).astype(BF16)
        o_ref[...] = jnp.zeros_like(o_ref)

    xn = xn_ref[...]
    h = _dot(xn, wg_ref[...])
    u = _dot(xn, wu_ref[...])
    a = (_silu(h) * u).astype(BF16)
    o_ref[...] += _dot(a, wo_ref[...])

    @pl.when(j == pl.num_programs(1) - 1)
    def _():
        o_ref[...] = x_ref[...] + 0.5 * _rms(o_ref[...], g1_ref[...])


def ffn_half_step(x, g0, g1, w_in, w_out):
    t, d = x.shape
    f = w_out.shape[0]
    tm = _pick(t, (512, 256, 128, 64, 32, 16, 8))
    tf = _pick(f, (512, 256, 128))
    nf = f // tf
    return pl.pallas_call(
        _ffn_kernel,
        grid=(t // tm, nf),
        in_specs=[
            pl.BlockSpec((tm, d), lambda i, j: (i, 0)),
            pl.BlockSpec((1, d), lambda i, j: (0, 0)),
            pl.BlockSpec((1, d), lambda i, j: (0, 0)),
            pl.BlockSpec((d, tf), lambda i, j: (0, j)),
            pl.BlockSpec((d, tf), lambda i, j: (0, j + nf)),
            pl.BlockSpec((tf, d), lambda i, j: (j, 0)),
        ],
        out_specs=pl.BlockSpec((tm, d), lambda i, j: (i, 0)),
        out_shape=jax.ShapeDtypeStruct((t, d), F32),
        scratch_shapes=[pltpu.VMEM((tm, d), BF16)],
        compiler_params=_cparams("parallel", "arbitrary"),
        name="ffn_half_step",
    )(x, g0.reshape(1, d), g1.reshape(1, d), w_in, w_in, w_out)


def _norm_matmul_kernel(x_ref, g_ref, w_ref, o_ref, xn_ref):
    @pl.when(pl.program_id(1) == 0)
    def _():
        xn_ref[...] = _rms(x_ref[...], g_ref[...]).astype(BF16)

    o_ref[...] = _dot(xn_ref[...], w_ref[...]).astype(o_ref.dtype)


def norm_matmul(x, g, w, out_dtype):
    t, d = x.shape
    n = w.shape[1]
    tm = _pick(t, (512, 256, 128, 64, 32, 16, 8))
    tn = _pick(n, (1024, 512, 256, 128))
    return pl.pallas_call(
        _norm_matmul_kernel,
        grid=(t // tm, n // tn),
        in_specs=[
            pl.BlockSpec((tm, d), lambda i, j: (i, 0)),
            pl.BlockSpec((1, d), lambda i, j: (0, 0)),
            pl.BlockSpec((d, tn), lambda i, j: (0, j)),
        ],
        out_specs=pl.BlockSpec((tm, tn), lambda i, j: (i, j)),
        out_shape=jax.ShapeDtypeStruct((t, n), out_dtype),
        scratch_shapes=[pltpu.VMEM((tm, d), BF16)],
        compiler_params=_cparams("parallel", "arbitrary"),
        name="norm_matmul",
    )(x, g.reshape(1, d), w)


def _matmul_norm_res_kernel(y_ref, w_ref, g_ref, x_ref, o_ref):
    k = pl.program_id(1)

    @pl.when(k == 0)
    def _():
        o_ref[...] = jnp.zeros_like(o_ref)

    o_ref[...] += _dot(y_ref[...], w_ref[...])

    @pl.when(k == pl.num_programs(1) - 1)
    def _():
        o_ref[...] = x_ref[...] + _rms(o_ref[...], g_ref[...])


def matmul_norm_res(y, w, g, x):
    t, kdim = y.shape
    d = w.shape[1]
    tm = _pick(t, (512, 256, 128, 64, 32, 16, 8))
    tk = _pick(kdim, (1024, 512, 256, 128))
    return pl.pallas_call(
        _matmul_norm_res_kernel,
        grid=(t // tm, kdim // tk),
        in_specs=[
            pl.BlockSpec((tm, tk), lambda i, k: (i, k)),
            pl.BlockSpec((tk, d), lambda i, k: (k, 0)),
            pl.BlockSpec((1, d), lambda i, k: (0, 0)),
            pl.BlockSpec((tm, d), lambda i, k: (i, 0)),
        ],
        out_specs=pl.BlockSpec((tm, d), lambda i, k: (i, 0)),
        out_shape=jax.ShapeDtypeStruct((t, d), F32),
        compiler_params=_cparams("parallel", "arbitrary"),
        name="matmul_norm_res",
    )(y, w, g.reshape(1, d), x)


ATTN_COL_CHUNK = 256
LANES = 128


ATTN_ROW_BLOCK = 64


def _diff_attn_kernel(slopes_ref, lam_ref, subln_ref, q_ref, k_ref, v_ref, o_ref,
                      s_ref, p_ref, wide_ref, m_ref, l_ref, acc_ref, *, lam_init, tq, tk):
    h = pl.program_id(1)
    qi = pl.program_id(2)
    ki = pl.program_id(3)
    dh = A_Q_HEAD_DIM
    cw = ATTN_COL_CHUNK
    rb = ATTN_ROW_BLOCK
    slope = slopes_ref[h]

    @pl.when(ki == 0)
    def _():
        m_ref[...] = jnp.full_like(m_ref, NEG_BIG)
        l_ref[...] = jnp.zeros_like(l_ref)
        acc_ref[...] = jnp.zeros_like(acc_ref)

    il = lax.broadcasted_iota(jnp.int32, (tq, 1), 0).astype(F32)
    jl = lax.broadcasted_iota(jnp.int32, (1, tk), 1).astype(F32)
    sigma = jnp.where(ki < qi, slope, jnp.where(ki > qi, -slope, 0.0))
    row_off = -sigma * (il + (qi * tq - ki * tk).astype(F32))

    def scores(on_diagonal):
        for m in range(2):
            q = q_ref[:, m * dh:(m + 1) * dh]
            for ci in range(tk // cw):
                cols = slice(ci * cw, (ci + 1) * cw)
                s_c = _dot_nt(q, k_ref[cols, m * dh:(m + 1) * dh])
                if on_diagonal:
                    s_c = s_c - slope * jnp.abs(il - jl[:, cols])
                else:
                    s_c = s_c + sigma * jl[:, cols]
                s_ref[m, :, cols] = s_c

    @pl.when(ki == qi)
    def _():
        scores(True)

    @pl.when(ki != qi)
    def _():
        scores(False)

    nparts = tk // LANES
    for m in range(2):
        for r in range(tq // rb):
            rows = slice(r * rb, (r + 1) * rb)
            part_max = s_ref[m, rows, 0:LANES]
            for pt in range(1, nparts):
                part_max = jnp.maximum(part_max, s_ref[m, rows, pt * LANES:(pt + 1) * LANES])
            wide_ref[m, rows, :] = part_max
        m_old = m_ref[m]
        m_new = jnp.maximum(m_old, jnp.max(wide_ref[m], axis=-1, keepdims=True) + row_off)
        alpha = jnp.exp2(m_old - m_new)
        m_ref[m] = m_new
        wide_ref[m] = jnp.broadcast_to(m_new - row_off, (tq, LANES))
        for r in range(tq // rb):
            rows = slice(r * rb, (r + 1) * rb)
            shift = wide_ref[m, rows, :]
            part_sum = None
            for pt in range(nparts):
                lanes = slice(pt * LANES, (pt + 1) * LANES)
                p = jnp.exp2(s_ref[m, rows, lanes] - shift)
                p_ref[m, rows, lanes] = p.astype(BF16)
                part_sum = p if part_sum is None else part_sum + p
            wide_ref[m, rows, :] = part_sum
        l_ref[m] = alpha * l_ref[m] + jnp.sum(wide_ref[m], axis=-1, keepdims=True)
        acc_ref[m] = alpha * acc_ref[m] + _dot(p_ref[m], v_ref[...])

    @pl.when(ki == pl.num_programs(3) - 1)
    def _():
        lf = lam_ref[...]
        lam_full = (jnp.exp(jnp.sum(lf[0:1] * lf[1:2], axis=-1, keepdims=True))
                    - jnp.exp(jnp.sum(lf[2:3] * lf[3:4], axis=-1, keepdims=True)) + lam_init)
        o = acc_ref[0] / l_ref[0] - lam_full * (acc_ref[1] / l_ref[1])
        o_ref[...] = (_rms(o, subln_ref[...]) * (1.0 - lam_init)).astype(o_ref.dtype)


def diff_attention_core(qkv, lam, subln, lam_init):
    b, s, d3 = qkv.shape
    d = d3 // 3
    hw = 2 * A_Q_HEAD_DIM
    nh = d // hw
    tq = _pick(s, (1024, 512, 256))
    tk = tq
    slopes = LOG2_E * 2.0 ** (-8.0 * jnp.arange(1, nh + 1, dtype=F32) / nh)
    kern = functools.partial(_diff_attn_kernel, lam_init=lam_init, tq=tq, tk=tk)
    return pl.pallas_call(
        kern,
        grid=(b, nh, s // tq, s // tk),
        in_specs=[
            pl.BlockSpec(memory_space=pltpu.SMEM),
            pl.BlockSpec((4, A_Q_HEAD_DIM), lambda bi, h, qi, ki: (0, 0)),
            pl.BlockSpec((1, hw), lambda bi, h, qi, ki: (0, 0)),
            pl.BlockSpec((None, tq, hw), lambda bi, h, qi, ki: (bi, qi, h)),
            pl.BlockSpec((None, tk, hw), lambda bi, h, qi, ki: (bi, ki, nh + h)),
            pl.BlockSpec((None, tk, hw), lambda bi, h, qi, ki: (bi, ki, 2 * nh + h)),
        ],
        out_specs=pl.BlockSpec((None, tq, hw), lambda bi, h, qi, ki: (bi, qi, h)),
        out_shape=jax.ShapeDtypeStruct((b, s, d), BF16),
        scratch_shapes=[
            pltpu.VMEM((2, tq, tk), F32),
            pltpu.VMEM((2, tq, tk), BF16),
            pltpu.VMEM((2, tq, LANES), F32),
            pltpu.VMEM((2, tq, 1), F32),
            pltpu.VMEM((2, tq, 1), F32),
            pltpu.VMEM((2, tq, hw), F32),
        ],
        compiler_params=_cparams("parallel", "parallel", "parallel", "arbitrary"),
        name="diff_attention_core",
    )(slopes, lam.astype(F32), subln.reshape(1, hw).astype(F32), qkv, qkv, qkv)


POOL_HALO = 16


def _pool_kernel(xp_ref, xc_ref, xn_ref, gin_ref, gout_ref, w_ref, scale_ref, o_ref, *, seq, tm):
    i = pl.program_id(1)
    n = pl.num_programs(1)
    gin = gin_ref[...]
    xc = xc_ref[...]
    uc = _rms(xc, gin)
    up = _rms(xp_ref[...], gin) * (i > 0).astype(F32)
    un = _rms(xn_ref[...], gin) * (i < n - 1).astype(F32)
    grp = w_ref.shape[1]
    hal = POOL_HALO

    r_c = lax.broadcasted_iota(jnp.int32, (tm, tm), 0)
    c_c = lax.broadcasted_iota(jnp.int32, (tm, tm), 1)
    d_c = c_c - r_c
    r_h = lax.broadcasted_iota(jnp.int32, (tm, hal), 0)
    c_h = lax.broadcasted_iota(jnp.int32, (tm, hal), 1)
    d_p = c_h - hal - r_h
    d_n = c_h + tm - r_h
    t_abs = i * tm + lax.broadcasted_iota(jnp.int32, (tm, 1), 0)

    outs = []
    for gi, win in enumerate(POOL_WINDOWS):
        half = win // 2
        sl = slice(gi * grp, (gi + 1) * grp)
        band_c = ((d_c >= -half) & (d_c < half)).astype(F32)
        band_p = (d_p >= -half).astype(F32)
        band_n = (d_n < half).astype(F32)
        wsum = (_dot_exact_lhs(band_c, uc[:, sl]) + _dot_exact_lhs(band_p, up[:, sl])
                + _dot_exact_lhs(band_n, un[:, sl]))
        lo = jnp.clip(t_abs - half, 0, seq)
        hi = jnp.clip(t_abs + half, 0, seq)
        count = (hi - lo).astype(F32)
        pooled = wsum / count - uc[:, sl]
        outs.append(_dot(pooled.astype(BF16), w_ref[gi]))
    y = jnp.concatenate(outs, axis=-1) * scale_ref[...]
    o_ref[...] = xc + _rms(y, gout_ref[...])


def pool_sublayer(x, g_in, g_out, w_groups, scale):
    b, s, d = x.shape
    tm = _pick(s, (256, 128, 64, 32, 16))
    hal = POOL_HALO
    nh = tm // hal
    last = s // hal - 1
    kern = functools.partial(_pool_kernel, seq=s, tm=tm)
    grp = w_groups.shape[1]
    return pl.pallas_call(
        kern,
        grid=(b, s // tm),
        in_specs=[
            pl.BlockSpec((None, hal, d), lambda bi, i: (bi, jnp.maximum(i * nh - 1, 0), 0)),
            pl.BlockSpec((None, tm, d), lambda bi, i: (bi, i, 0)),
            pl.BlockSpec((None, hal, d), lambda bi, i: (bi, jnp.minimum((i + 1) * nh, last), 0)),
            pl.BlockSpec((1, d), lambda bi, i: (0, 0)),
            pl.BlockSpec((1, d), lambda bi, i: (0, 0)),
            pl.BlockSpec((len(POOL_WINDOWS), grp, grp), lambda bi, i: (0, 0, 0)),
            pl.BlockSpec((1, d), lambda bi, i: (0, 0)),
        ],
        out_specs=pl.BlockSpec((None, tm, d), lambda bi, i: (bi, i, 0)),
        out_shape=jax.ShapeDtypeStruct((b, s, d), F32),
        compiler_params=_cparams("parallel", "parallel"),
        name="pool_sublayer",
    )(x, x, x, g_in.reshape(1, d), g_out.reshape(1, d), w_groups, scale.reshape(1, d).astype(F32))


CONV_HALO = 16


def _gdn_conv_kernel(xp_ref, xc_ref, xn_ref, w_ref, o_ref, ext_ref, *, tb, n_q_blocks, n_qk_blocks):
    i = pl.program_id(1)
    j = pl.program_id(2)
    n = pl.num_programs(1)
    hal = CONV_HALO
    left = GDN_CONV // 2
    ext_ref[0:hal, :] = xp_ref[...].astype(F32) * (i > 0).astype(F32)
    ext_ref[hal:hal + tb, :] = xc_ref[...].astype(F32)
    ext_ref[hal + tb:hal + tb + hal, :] = xn_ref[...].astype(F32) * (i < n - 1).astype(F32)
    acc = None
    for kk in range(GDN_CONV):
        start = hal - left + kk
        term = ext_ref[start:start + tb, :] * w_ref[kk:kk + 1, :]
        acc = term if acc is None else acc + term
    y = _silu(acc)

    @pl.when(j >= n_qk_blocks)
    def _():
        o_ref[...] = y.astype(o_ref.dtype)

    @pl.when(j < n_qk_blocks)
    def _():
        qscale = jnp.where(j < n_q_blocks, GDN_K_DIM ** -0.5, 1.0).astype(F32)
        for hh in range(y.shape[1] // GDN_K_DIM):
            sl = slice(hh * GDN_K_DIM, (hh + 1) * GDN_K_DIM)
            t = y[:, sl]
            nrm = lax.rsqrt(jnp.sum(t * t, axis=-1, keepdims=True) + RMS_EPS) * qscale
            o_ref[:, sl] = (t * nrm).astype(o_ref.dtype)


def gdn_conv(qkvz, conv_w, qk_w, conv_dim):
    b, s, _ = qkvz.shape
    tb = _pick(s, (256, 128, 64, 32, 16))
    cb = _pick(qk_w, (1024, 512, 256, 128))
    hal = CONV_HALO
    nh = tb // hal
    last = s // hal - 1
    kern = functools.partial(_gdn_conv_kernel, tb=tb, n_q_blocks=qk_w // cb, n_qk_blocks=2 * qk_w // cb)
    return pl.pallas_call(
        kern,
        grid=(b, s // tb, conv_dim // cb),
        in_specs=[
            pl.BlockSpec((None, hal, cb), lambda bi, i, j: (bi, jnp.maximum(i * nh - 1, 0), j)),
            pl.BlockSpec((None, tb, cb), lambda bi, i, j: (bi, i, j)),
            pl.BlockSpec((None, hal, cb), lambda bi, i, j: (bi, jnp.minimum((i + 1) * nh, last), j)),
            pl.BlockSpec((GDN_CONV, cb), lambda bi, i, j: (0, j)),
        ],
        out_specs=pl.BlockSpec((None, tb, cb), lambda bi, i, j: (bi, i, j)),
        out_shape=jax.ShapeDtypeStruct((b, s, conv_dim), BF16),
        scratch_shapes=[pltpu.VMEM((tb + 2 * hal, cb), F32)],
        compiler_params=_cparams("parallel", "parallel", "parallel"),
        name="gdn_conv",
    )(qkvz, qkvz, qkvz, conv_w.astype(F32))


GDN_HEAD_GROUP = 4


def _split2(x):
    hi = x.astype(BF16)
    lo = (x - hi.astype(F32)).astype(BF16)
    return hi, lo


def _dot_hi_pieces(a_pieces, b_pieces):
    ah, al = a_pieces
    bh, bl = b_pieces
    return _dot(ah, bh) + _dot(ah, bl) + _dot(al, bh)


def _gdn_core_kernel(q_ref, k_ref, v_ref, ab_ref, alog_ref, dt_ref, o_ref, s_ref, *,
                     direction, tb, n_vheads, hg):
    grp = pl.program_id(1)
    blk = pl.program_id(2)
    c = GDN_CHUNK
    nchunk = tb // c
    dk = GDN_K_DIM
    dv = GDN_V_DIM
    assert dk == 2 * c and dv == dk
    hv2 = 2 * n_vheads
    nvh = 2 * hg
    abw = ab_ref.shape[1]

    @pl.when(blk == 0)
    def _():
        s_ref[...] = jnp.zeros_like(s_ref)

    ab = ab_ref[...]
    lane = lax.broadcasted_iota(jnp.int32, ab.shape, 1)
    g_all = -jnp.exp(alog_ref[...]) * jax.nn.softplus(ab + dt_ref[...])
    beta_all = jax.nn.sigmoid(ab)

    r_t = lax.broadcasted_iota(jnp.int32, (tb, tb), 0)
    c_t = lax.broadcasted_iota(jnp.int32, (tb, tb), 1)
    same_chunk = _div_pow2(r_t, c) == _div_pow2(c_t, c)
    tri = (c_t <= r_t) if direction == 0 else (c_t >= r_t)
    cum_mat = (same_chunk & tri).astype(F32)
    gc_all = _dot_exact_lhs(cum_mat, jnp.where(lane < hv2, g_all, 0.0))

    col0 = direction * n_vheads + nvh * grp
    sel_r = lax.broadcasted_iota(jnp.int32, (abw, nvh * dv), 0)
    sel_c = lax.broadcasted_iota(jnp.int32, (abw, nvh * dv), 1)
    a_col = col0 + _div_pow2(sel_c, dv)
    gcb_all = _dot_exact_rhs(gc_all, (sel_r == a_col).astype(F32))
    bb_all = _dot_exact_rhs(beta_all, (sel_r == a_col + hv2).astype(F32))

    gc_dup = jnp.concatenate([gc_all[(i // 2) * c:(i // 2 + 1) * c] for i in range(2 * nchunk)], axis=0)
    nrow = max(16, nvh)
    rs_r = lax.broadcasted_iota(jnp.int32, (nrow, abw), 0)
    rs_c = lax.broadcasted_iota(jnp.int32, (nrow, abw), 1)
    rsel = (rs_c == col0 + rs_r).astype(BF16)
    d_hi, d_mid, d_lo = _split3(gc_dup)
    rows_g = _dot_nt(rsel, d_hi) + _dot_nt(rsel, d_mid) + _dot_nt(rsel, d_lo)

    ii = lax.broadcasted_iota(jnp.int32, (c, 2 * c), 0)
    ll = lax.broadcasted_iota(jnp.int32, (c, 2 * c), 1)
    jj = jnp.bitwise_and(ll, c - 1)
    second = ll >= c
    if direction == 0:
        m_incl, m_strict, last_row = ii >= jj, ii > jj, c - 1
    else:
        m_incl, m_strict, last_row = ii <= jj, ii < jj, 0
    eye_p = (ii == jj).astype(F32)
    bd_r = lax.broadcasted_iota(jnp.int32, (2 * c, 2 * c), 0)
    bd_c = lax.broadcasted_iota(jnp.int32, (2 * c, 2 * c), 1)
    bd_mask = (bd_r >= c) == (bd_c >= c)
    second_row = lax.broadcasted_iota(jnp.int32, (1, 2 * c), 1) >= c
    lane_sol = lax.broadcasted_iota(jnp.int32, (c, 4 * dv), 1)
    sol_second = jnp.bitwise_and(_div_pow2(lane_sol, dv), 1) == 1
    bd2_r = lax.broadcasted_iota(jnp.int32, (2 * c, 2 * dv), 0)
    bd2_c = lax.broadcasted_iota(jnp.int32, (2 * c, 2 * dv), 1)
    bd2_mask = (bd2_r >= c) == (bd2_c >= dv)
    st_r = lax.broadcasted_iota(jnp.int32, (2 * dk, 2 * dv), 0)
    st_c = lax.broadcasted_iota(jnp.int32, (2 * dk, 2 * dv), 1)
    st_mask = (st_r >= dk) == (st_c >= dv)
    st_second = lax.broadcasted_iota(jnp.int32, (2 * dk, dv), 0) >= dk

    def block_diag(piece):
        return jnp.where(bd_mask, jnp.concatenate([piece, piece], axis=0), 0)

    order = list(range(nchunk)) if direction == 0 else list(range(nchunk - 1, -1, -1))
    probs = [(ch, hh) for ch in order for hh in range(hg)]

    pre = {}
    for ch, hh in probs:
        rows = slice(ch * c, (ch + 1) * c)
        k = k_ref[rows, hh * dk:(hh + 1) * dk]
        q = q_ref[rows, hh * dk:(hh + 1) * dk]
        k2 = jnp.concatenate([k, k], axis=0)
        gram = _dot_nt(jnp.concatenate([k, q], axis=0), k2)
        kk_p = gram[:c]
        qk_p = gram[c:]
        g_full = gcb_all[rows, 2 * hh * dv:(2 * hh + 2) * dv]
        b_full = bb_all[rows, 2 * hh * dv:(2 * hh + 2) * dv]
        g_p = jnp.where(second, g_full[:, dv:dv + 2 * c], g_full[:, :2 * c])
        b_p = jnp.where(second, b_full[:, dv:dv + 2 * c], b_full[:, :2 * c])
        r_blk = slice(ch * 2 * c, (ch + 1) * 2 * c)
        r_row = jnp.where(second_row, rows_g[2 * hh + 1:2 * hh + 2, r_blk], rows_g[2 * hh:2 * hh + 1, r_blk])
        decay = jnp.exp(jnp.where(m_incl, g_p - r_row, NEG_BIG))
        a_p = jnp.where(m_strict, b_p * kk_p * decay, 0.0)
        qkm_p = jnp.where(m_incl, qk_p * decay, 0.0)
        kf2 = jnp.concatenate([k, k], axis=1).astype(F32)
        qf2 = jnp.concatenate([q, q], axis=1).astype(F32)
        eg = jnp.exp(g_full)
        vb = v_ref[rows, hh * 2 * dv:(hh + 1) * 2 * dv].astype(F32) * b_full
        kb = kf2 * (b_full * eg)
        x = jnp.concatenate([vb, kb], axis=1)
        rhs = jnp.concatenate([jnp.where(sol_second, 0.0, x), jnp.where(sol_second, x, 0.0)],
                              axis=0).astype(BF16)
        g_last = g_full[last_row:last_row + 1, :]
        pre[(ch, hh)] = dict(mp=-a_p, t=eye_p - a_p, qkm=qkm_p.astype(BF16), rhs=rhs,
                             qs=(qf2 * eg).astype(BF16), kd=(kf2 * jnp.exp(g_last - g_full)).astype(BF16),
                             egl=jnp.exp(g_last))

    def split_and_diag(p):
        p['mpc'] = _split2(p['mp'])
        p['mbd'] = tuple(block_diag(x) for x in p['mpc'])

    for key in probs:
        split_and_diag(pre[key])
    for key in probs:
        p = pre[key]
        p['mp'] = _dot_hi_pieces(p['mpc'], p['mbd'])
    n_levels = int(math.log2(c)) - 1
    for lvl in range(n_levels):
        last = lvl == n_levels - 1
        for key in probs:
            split_and_diag(pre[key])
        for key in probs:
            p = pre[key]
            mh, ml = p['mpc']
            bh, bl = p['mbd']
            th, tl = _split2(p['t'])
            if last:
                p['t'] = p['t'] + _dot_hi_pieces((th, tl), (bh, bl))
            else:
                by_hi = _dot(jnp.concatenate([mh, ml, th, tl], axis=0), bh)
                by_lo = _dot(jnp.concatenate([mh, th], axis=0), bl)
                p['mp'] = by_hi[:c] + by_hi[c:2 * c] + by_lo[:c]
                p['t'] = p['t'] + (by_hi[2 * c:3 * c] + by_hi[3 * c:] + by_lo[c:])
    for key in probs:
        p = pre[key]
        sol = _dot(p['t'].astype(BF16), p['rhs'])
        p['u'] = sol[:, :2 * dv]
        p['w'] = sol[:, 2 * dv:].astype(BF16)

    for ch in order:
        rows = slice(ch * c, (ch + 1) * c)
        states = [s_ref[hh] for hh in range(hg)]
        sbs = [jnp.where(st_mask, jnp.concatenate([s, s], axis=1), 0.0).astype(BF16) for s in states]
        wq_s = [_dot(jnp.concatenate([pre[(ch, hh)]['w'], pre[(ch, hh)]['qs']], axis=0), sbs[hh])
                for hh in range(hg)]
        for hh in range(hg):
            p = pre[(ch, hh)]
            vnb = (p['u'] - wq_s[hh][:c]).astype(BF16)
            vn2 = jnp.where(bd2_mask, jnp.concatenate([vnb, vnb], axis=0), 0)
            o = wq_s[hh][c:] + _dot(p['qkm'], vn2)
            o_ref[rows, hh * 2 * dv:(hh + 1) * 2 * dv] = o.astype(o_ref.dtype)
            upd = _dot_tn(p['kd'], vnb)
            upd_d = jnp.where(st_second, upd[:, dv:], upd[:, :dv])
            egl = p['egl']
            dl = jnp.where(st_second, jnp.broadcast_to(egl[:, dv:], (2 * dk, dv)),
                           jnp.broadcast_to(egl[:, :dv], (2 * dk, dv)))
            s_ref[hh] = states[hh] * dl + upd_d


def gdn_core(qkv, ab, a_log, dt_bias, direction, n_kheads, n_vheads):
    b, s, _ = qkv.shape
    tb = _pick(s, (256, 128, 64))
    nblk = s // tb
    abw = ab.shape[-1]
    hg = _pick(n_kheads, (GDN_HEAD_GROUP, 2, 1))
    ngrp = n_kheads // hg
    pad = jnp.zeros((2 * n_vheads,), F32)
    alog_row = jnp.concatenate([a_log.reshape(-1).astype(F32), pad]).reshape(1, abw)
    dt_row = jnp.concatenate([dt_bias.reshape(-1).astype(F32), pad]).reshape(1, abw)

    def tok(blk):
        return blk if direction == 0 else nblk - 1 - blk

    kern = functools.partial(_gdn_core_kernel, direction=direction, tb=tb, n_vheads=n_vheads, hg=hg)
    qw = hg * GDN_K_DIM
    vw = hg * 2 * GDN_V_DIM
    return pl.pallas_call(
        kern,
        grid=(b, ngrp, nblk),
        in_specs=[
            pl.BlockSpec((None, tb, qw), lambda bi, g, blk: (bi, tok(blk), g)),
            pl.BlockSpec((None, tb, qw), lambda bi, g, blk: (bi, tok(blk), ngrp + g)),
            pl.BlockSpec((None, tb, vw), lambda bi, g, blk: (bi, tok(blk), ngrp + g)),
            pl.BlockSpec((None, tb, abw), lambda bi, g, blk: (bi, tok(blk), 0)),
            pl.BlockSpec((1, abw), lambda bi, g, blk: (0, 0)),
            pl.BlockSpec((1, abw), lambda bi, g, blk: (0, 0)),
        ],
        out_specs=pl.BlockSpec((None, tb, vw), lambda bi, g, blk: (bi, tok(blk), g)),
        out_shape=jax.ShapeDtypeStruct((b, s, n_vheads * GDN_V_DIM), F32),
        scratch_shapes=[pltpu.VMEM((hg, 2 * GDN_K_DIM, GDN_V_DIM), F32)],
        compiler_params=_cparams("parallel", "parallel", "arbitrary"),
        name="gdn_core_fwd" if direction == 0 else "gdn_core_bwd",
    )(qkv, qkv, qkv, ab, alog_row, dt_row)


def _gdn_gate_kernel(of_ref, ob_ref, z_ref, g_ref, o_ref):
    g = g_ref[...]
    for hh in range(of_ref.shape[1] // GDN_V_DIM):
        sl = slice(hh * GDN_V_DIM, (hh + 1) * GDN_V_DIM)
        o = of_ref[:, sl] + ob_ref[:, sl]
        z = z_ref[:, sl].astype(F32)
        o_ref[:, sl] = (_rms(o, g) * _silu(z)).astype(o_ref.dtype)


def gdn_gate(o_fwd, o_bwd, qkvz, norm_g, z_col0):
    t, w = o_fwd.shape
    tm = _pick(t, (512, 256, 128, 64, 32, 16))
    cb = _pick(w, (1024, 512, 256, 128))
    zoff = z_col0 // cb
    return pl.pallas_call(
        _gdn_gate_kernel,
        grid=(t // tm, w // cb),
        in_specs=[
            pl.BlockSpec((tm, cb), lambda i, j: (i, j)),
            pl.BlockSpec((tm, cb), lambda i, j: (i, j)),
            pl.BlockSpec((tm, cb), lambda i, j: (i, zoff + j)),
            pl.BlockSpec((1, GDN_V_DIM), lambda i, j: (0, 0)),
        ],
        out_specs=pl.BlockSpec((tm, cb), lambda i, j: (i, j)),
        out_shape=jax.ShapeDtypeStruct((t, w), BF16),
        compiler_params=_cparams("parallel", "parallel"),
        name="gdn_gate",
    )(o_fwd, o_bwd, qkvz, norm_g.reshape(1, GDN_V_DIM).astype(F32))


def _na_kernel(bias_ref, q_ref, kp_ref, kc_ref, kn_ref, vp_ref, vc_ref, vn_ref, o_ref, *, grid_rows):
    i = pl.program_id(2)
    rb = NA_ROW_BLOCK
    tq = rb * GRID_W
    q = q_ref[...]
    s = jnp.concatenate([_dot_nt(q, kp_ref[...]), _dot_nt(q, kc_ref[...]), _dot_nt(q, kn_ref[...])],
                        axis=-1)
    q_row = i * rb + _div_pow2(lax.broadcasted_iota(jnp.int32, (tq, 1), 0), GRID_W)
    win_lo = jnp.clip(q_row - NA_WIN_ROWS // 2, 0, grid_rows - NA_WIN_ROWS)
    k_row = (i - 1) * rb + _div_pow2(lax.broadcasted_iota(jnp.int32, (1, 3 * tq), 1), GRID_W)
    valid = (k_row >= win_lo) & (k_row < win_lo + NA_WIN_ROWS)
    s = jnp.where(valid, s + bias_ref[...], NEG_BIG)
    m = jnp.max(s, axis=-1, keepdims=True)
    p = jnp.exp(s - m)
    l = jnp.sum(p, axis=-1, keepdims=True)
    pb = p.astype(BF16)
    o = (_dot(pb[:, :tq], vp_ref[...]) + _dot(pb[:, tq:2 * tq], vc_ref[...])
         + _dot(pb[:, 2 * tq:], vn_ref[...]))
    o_ref[...] = (o / l).astype(o_ref.dtype)


def _na_dense_bias(rpb):
    h = rpb.shape[0]
    rb, w, kw, khw = NA_ROW_BLOCK, GRID_W, NA_WIN_COLS, NA_WIN_ROWS
    cols = jnp.arange(w)
    col_start = jnp.clip(cols - kw // 2, 0, w - kw)
    dc = cols[None, :] - cols[:, None]
    col_ok = (cols[None, :] >= col_start[:, None]) & (cols[None, :] < col_start[:, None] + kw)
    tiles = rpb[:, :, jnp.clip(dc + kw - 1, 0, 2 * kw - 2)]
    tiles = jnp.where(col_ok[None, None], tiles.astype(F32), NEG_BIG)
    tiles = jnp.concatenate([tiles, jnp.full((h, 1, w, w), NEG_BIG, F32)], axis=1)
    rq = jnp.arange(rb)[:, None]
    rk = jnp.arange(3 * rb)[None, :]
    dr = rk - rb - rq
    idx = jnp.where(jnp.abs(dr) <= khw - 1, dr + khw - 1, 2 * khw - 1)
    dense = tiles[:, idx]
    dense = dense.transpose(0, 1, 3, 2, 4).reshape(h, rb * w, 3 * rb * w)
    return dense


def na_core(qkv, rpb):
    b, s, d3 = qkv.shape
    d = d3 // 3
    dh = NA_HEAD_DIM
    nh = d // dh
    grid_rows = s // GRID_W
    tq = NA_ROW_BLOCK * GRID_W
    nblk = s // tq
    bias = _na_dense_bias(rpb)

    def prev(i):
        return jnp.maximum(i - 1, 0)

    def nxt(i):
        return jnp.minimum(i + 1, nblk - 1)

    kern = functools.partial(_na_kernel, grid_rows=grid_rows)
    return pl.pallas_call(
        kern,
        grid=(b, nh, nblk),
        in_specs=[
            pl.BlockSpec((None, tq, 3 * tq), lambda bi, h, i: (h, 0, 0)),
            pl.BlockSpec((None, tq, dh), lambda bi, h, i: (bi, i, h)),
            pl.BlockSpec((None, tq, dh), lambda bi, h, i: (bi, prev(i), nh + h)),
            pl.BlockSpec((None, tq, dh), lambda bi, h, i: (bi, i, nh + h)),
            pl.BlockSpec((None, tq, dh), lambda bi, h, i: (bi, nxt(i), nh + h)),
            pl.BlockSpec((None, tq, dh), lambda bi, h, i: (bi, prev(i), 2 * nh + h)),
            pl.BlockSpec((None, tq, dh), lambda bi, h, i: (bi, i, 2 * nh + h)),
            pl.BlockSpec((None, tq, dh), lambda bi, h, i: (bi, nxt(i), 2 * nh + h)),
        ],
        out_specs=pl.BlockSpec((None, tq, dh), lambda bi, h, i: (bi, i, h)),
        out_shape=jax.ShapeDtypeStruct((b, s, d), BF16),
        compiler_params=_cparams("parallel", "parallel", "parallel"),
        name="na_core",
    )(bias, qkv, qkv, qkv, qkv, qkv, qkv, qkv)


def _lambda_init(layer_idx):
    return 0.8 - 0.6 * math.exp(-0.3 * layer_idx)


def _scaled_q_weight(w_qkv, scale):
    d = w_qkv.shape[1] // 3
    return jnp.concatenate([w_qkv[:, :d] * scale, w_qkv[:, d:]], axis=1).astype(BF16)


def _trunk(x, p):
    b, s, d = x.shape
    depth = p['norms'].shape[0]
    xt = x.reshape(b * s, d)
    for i in range(depth):
        g = p['norms'][i]
        xt = ffn_half_step(xt, g[0], g[1], p['ffn_w_in'][i][0], p['ffn_w_out'][i][0])
        kind, j = i % N_MIXERS, i // N_MIXERS
        if kind == 0:
            w_qkv = _scaled_q_weight(p['a_w_qkv'][j], LOG2_E * A_Q_HEAD_DIM ** -0.5)
            qkv = norm_matmul(xt, g[2], w_qkv, BF16).reshape(b, s, 3 * d)
            o = diff_attention_core(qkv, p['a_lambda'][j], p['a_subln'][j], _lambda_init(i))
            xt = matmul_norm_res(o.reshape(b * s, d), p['a_w_out'][j].astype(BF16), g[3], xt)
        elif kind == 1:
            xt = pool_sublayer(xt.reshape(b, s, d), g[2], g[3], p['pool_w'][j].astype(BF16),
                               p['pool_scale'][j]).reshape(b * s, d)
        elif kind == 2:
            n_vheads = p['gdn_a_log'].shape[-1]
            v_w = n_vheads * GDN_V_DIM
            w_in = p['gdn_w_in'][j]
            conv_dim = p['gdn_conv'].shape[-1]
            qk_w = (conv_dim - v_w) // 2
            n_kheads = qk_w // GDN_K_DIM
            qkvz = norm_matmul(xt, g[2], w_in[:, :conv_dim + v_w].astype(BF16), BF16)
            ab = norm_matmul(xt, g[2], w_in[:, conv_dim + v_w:].astype(BF16), F32)
            qkv = gdn_conv(qkvz.reshape(b, s, conv_dim + v_w), p['gdn_conv'][j], qk_w, conv_dim)
            ab3 = ab.reshape(b, s, 4 * n_vheads)
            o_f = gdn_core(qkv, ab3, p['gdn_a_log'][j], p['gdn_dt_bias'][j], 0, n_kheads, n_vheads)
            o_b = gdn_core(qkv, ab3, p['gdn_a_log'][j], p['gdn_dt_bias'][j], 1, n_kheads, n_vheads)
            gated = gdn_gate(o_f.reshape(b * s, v_w), o_b.reshape(b * s, v_w), qkvz,
                             p['gdn_norm'][j], conv_dim)
            xt = matmul_norm_res(gated, p['gdn_w_out'][j].astype(BF16), g[3], xt)
        else:
            w_qkv = _scaled_q_weight(p['na_w_qkv'][j], NA_HEAD_DIM ** -0.5)
            qkv = norm_matmul(xt, g[2], w_qkv, BF16).reshape(b, s, 3 * d)
            o = na_core(qkv, p['na_rpb'][j])
            xt = matmul_norm_res(o.reshape(b * s, d), p['na_w_out'][j].astype(BF16), g[3], xt)
        xt = ffn_half_step(xt, g[4], g[5], p['ffn_w_in'][i][1], p['ffn_w_out'][i][1])
    return xt.reshape(b, s, d)


def kernel(x_prompt, x_sample, norms, ffn_w_in, ffn_w_out, a_w_qkv, a_lambda, a_subln, a_w_out, pool_w, pool_scale, gdn_w_in, gdn_conv, gdn_a_log, gdn_dt_bias, gdn_norm, gdn_w_out, na_w_qkv, na_rpb, na_w_out):
    p = dict(norms=norms, ffn_w_in=ffn_w_in.astype(BF16), ffn_w_out=ffn_w_out.astype(BF16),
             a_w_qkv=a_w_qkv, a_lambda=a_lambda, a_subln=a_subln, a_w_out=a_w_out, pool_w=pool_w,
             pool_scale=pool_scale, gdn_w_in=gdn_w_in, gdn_conv=gdn_conv, gdn_a_log=gdn_a_log,
             gdn_dt_bias=gdn_dt_bias, gdn_norm=gdn_norm, gdn_w_out=gdn_w_out, na_w_qkv=na_w_qkv,
             na_rpb=na_rpb, na_w_out=na_w_out)
    return (_trunk(x_prompt, p), _trunk(x_sample, p))
```

```python
import functools
import math

import jax
import jax.numpy as jnp
from jax import lax
from jax.experimental import pallas as pl
from jax.experimental.pallas import tpu as pltpu

F32 = jnp.float32
BF16 = jnp.bfloat16

RMS_EPS = 1e-6
LOG2_E = math.log2(math.e)
N_MIXERS = 4
GRID_W = 64
A_Q_HEAD_DIM = 128
POOL_WINDOWS = (2, 4, 8, 16)
GDN_K_DIM = 128
GDN_V_DIM = 128
GDN_CONV = 4
GDN_CHUNK = 64
NA_HEAD_DIM = 128
NA_WIN_ROWS = 8
NA_WIN_COLS = 16
NA_ROW_BLOCK = 8
NEG_BIG = -1e30

VMEM_LIMIT_BYTES = 56 * 1024 * 1024


def _cparams(*sem):
    return pltpu.CompilerParams(dimension_semantics=sem, vmem_limit_bytes=VMEM_LIMIT_BYTES)


def _rms(x, g):
    ms = jnp.mean(x * x, axis=-1, keepdims=True)
    return x * lax.rsqrt(ms + RMS_EPS) * g


def _silu(x):
    return x * jax.nn.sigmoid(x)


def _dot(a, b):
    return jnp.dot(a, b, preferred_element_type=F32)


def _dot_nt(a, b):
    return lax.dot_general(a, b, (((1,), (1,)), ((), ())), preferred_element_type=F32)


def _dot_tn(a, b):
    return lax.dot_general(a, b, (((0,), (0,)), ((), ())), preferred_element_type=F32)


def _split3(x):
    hi = x.astype(BF16)
    r1 = x - hi.astype(F32)
    mid = r1.astype(BF16)
    lo = (r1 - mid.astype(F32)).astype(BF16)
    return hi, mid, lo


def _dot_exact_lhs(a_exact, x):
    hi, mid, lo = _split3(x)
    a = a_exact.astype(BF16)
    return _dot(a, hi) + _dot(a, mid) + _dot(a, lo)


def _dot_exact_rhs(x, b_exact):
    hi, mid, lo = _split3(x)
    b = b_exact.astype(BF16)
    return _dot(hi, b) + _dot(mid, b) + _dot(lo, b)


def _dot_hi(a, b):
    ah = a.astype(BF16)
    al = (a - ah.astype(F32)).astype(BF16)
    bh = b.astype(BF16)
    bl = (b - bh.astype(F32)).astype(BF16)
    return _dot(ah, bh) + _dot(ah, bl) + _dot(al, bh)


def _div_pow2(x, n):
    assert n & (n - 1) == 0
    return lax.shift_right_logical(x, int(math.log2(n)))


def _pick(n, prefs):
    for p in prefs:
        if n % p == 0:
            return p
    return n


def _ffn_kernel(x_ref, g0_ref, g1_ref, wg_ref, wu_ref, wo_ref, o_ref, xn_ref):
    j = pl.program_id(1)

    @pl.when(j == 0)
    def _():
        xn_ref[...] = _rms(x_ref[...], g0_ref[...]).astype(BF16)
        o_ref[...] = jnp.zeros_like(o_ref)

    xn = xn_ref[...]
    h = _dot(xn, wg_ref[...])
    u = _dot(xn, wu_ref[...])
    a = (_silu(h) * u).astype(BF16)
    o_ref[...] += _dot(a, wo_ref[...])

    @pl.when(j == pl.num_programs(1) - 1)
    def _():
        o_ref[...] = x_ref[...] + 0.5 * _rms(o_ref[...], g1_ref[...])


def ffn_half_step(x, g0, g1, w_in, w_out):
    t, d = x.shape
    f = w_out.shape[0]
    tm = _pick(t, (512, 256, 128, 64, 32, 16, 8))
    tf = _pick(f, (512, 256, 128))
    nf = f // tf
    return pl.pallas_call(
        _ffn_kernel,
        grid=(t // tm, nf),
        in_specs=[
            pl.BlockSpec((tm, d), lambda i, j: (i, 0)),
            pl.BlockSpec((1, d), lambda i, j: (0, 0)),
            pl.BlockSpec((1, d), lambda i, j: (0, 0)),
            pl.BlockSpec((d, tf), lambda i, j: (0, j)),
            pl.BlockSpec((d, tf), lambda i, j: (0, j + nf)),
            pl.BlockSpec((tf, d), lambda i, j: (j, 0)),
        ],
        out_specs=pl.BlockSpec((tm, d), lambda i, j: (i, 0)),
        out_shape=jax.ShapeDtypeStruct((t, d), F32),
        scratch_shapes=[pltpu.VMEM((tm, d), BF16)],
        compiler_params=_cparams("parallel", "arbitrary"),
        name="ffn_half_step",
    )(x, g0.reshape(1, d), g1.reshape(1, d), w_in, w_in, w_out)


def _norm_matmul_kernel(x_ref, g_ref, w_ref, o_ref, xn_ref):
    @pl.when(pl.program_id(1) == 0)
    def _():
        xn_ref[...] = _rms(x_ref[...], g_ref[...]).astype(BF16)

    o_ref[...] = _dot(xn_ref[...], w_ref[...]).astype(o_ref.dtype)


def norm_matmul(x, g, w, out_dtype):
    t, d = x.shape
    n = w.shape[1]
    tm = _pick(t, (512, 256, 128, 64, 32, 16, 8))
    tn = _pick(n, (1024, 512, 256, 128))
    return pl.pallas_call(
        _norm_matmul_kernel,
        grid=(t // tm, n // tn),
        in_specs=[
            pl.BlockSpec((tm, d), lambda i, j: (i, 0)),
            pl.BlockSpec((1, d), lambda i, j: (0, 0)),
            pl.BlockSpec((d, tn), lambda i, j: (0, j)),
        ],
        out_specs=pl.BlockSpec((tm, tn), lambda i, j: (i, j)),
        out_shape=jax.ShapeDtypeStruct((t, n), out_dtype),
        scratch_shapes=[pltpu.VMEM((tm, d), BF16)],
        compiler_params=_cparams("parallel", "arbitrary"),
        name="norm_matmul",
    )(x, g.reshape(1, d), w)


def _matmul_norm_res_kernel(y_ref, w_ref, g_ref, x_ref, o_ref):
    k = pl.program_id(1)

    @pl.when(k == 0)
    def _():
        o_ref[...] = jnp.zeros_like(o_ref)

    o_ref[...] += _dot(y_ref[...], w_ref[...])

    @pl.when(k == pl.num_programs(1) - 1)
    def _():
        o_ref[...] = x_ref[...] + _rms(o_ref[...], g_ref[...])


def matmul_norm_res(y, w, g, x):
    t, kdim = y.shape
    d = w.shape[1]
    tm = _pick(t, (512, 256, 128, 64, 32, 16, 8))
    tk = _pick(kdim, (1024, 512, 256, 128))
    return pl.pallas_call(
        _matmul_norm_res_kernel,
        grid=(t // tm, kdim // tk),
        in_specs=[
            pl.BlockSpec((tm, tk), lambda i, k: (i, k)),
            pl.BlockSpec((tk, d), lambda i, k: (k, 0)),
            pl.BlockSpec((1, d), lambda i, k: (0, 0)),
            pl.BlockSpec((tm, d), lambda i, k: (i, 0)),
        ],
        out_specs=pl.BlockSpec((tm, d), lambda i, k: (i, 0)),
        out_shape=jax.ShapeDtypeStruct((t, d), F32),
        compiler_params=_cparams("parallel", "arbitrary"),
        name="matmul_norm_res",
    )(y, w, g.reshape(1, d), x)


ATTN_Q_TILE = 1024
ATTN_K_TILE = 1024
ATTN_COL_CHUNK = 256
LANES = 128


ATTN_ROW_BLOCK = 64
ATTN_PV_ROWS = 256


def _diff_attn_kernel(slopes_ref, lam_ref, subln_ref, q_ref, k_ref, v_ref, o_ref,
                      s0_ref, s1_ref, p0_ref, p1_ref, w0_ref, w1_ref, m_ref, l_ref, acc_ref, *,
                      lam_init, tq, tk):
    s_refs, p_refs, w_refs = (s0_ref, s1_ref), (p0_ref, p1_ref), (w0_ref, w1_ref)
    h = pl.program_id(1)
    qi = pl.program_id(2)
    ki = pl.program_id(3)
    dh = A_Q_HEAD_DIM
    cw = ATTN_COL_CHUNK
    rb = ATTN_ROW_BLOCK
    slope = slopes_ref[h]
    nparts = tk // LANES
    il = lax.broadcasted_iota(jnp.int32, (tq, 1), 0).astype(F32)
    jl = lax.broadcasted_iota(jnp.int32, (1, tk), 1).astype(F32)

    @pl.when(ki == 0)
    def _():
        m_ref[...] = jnp.full_like(m_ref, NEG_BIG)
        l_ref[...] = jnp.zeros_like(l_ref)
        acc_ref[...] = jnp.zeros_like(acc_ref)

    delta = qi * tq - ki * tk
    before = delta >= tk
    after = delta <= -tq
    sigma = jnp.where(before, slope, jnp.where(after, -slope, 0.0))
    row_off = -sigma * (il + delta.astype(F32))

    def scores(overlapping):
        for m in range(2):
            q = q_ref[:, m * dh:(m + 1) * dh]
            for ci in range(tk // cw):
                cols = slice(ci * cw, (ci + 1) * cw)
                s_c = _dot_nt(q, k_ref[cols, m * dh:(m + 1) * dh])
                if overlapping:
                    s_c = s_c - slope * jnp.abs((il + delta.astype(F32)) - jl[:, cols])
                else:
                    s_c = s_c + sigma * jl[:, cols]
                s_refs[m][:, cols] = s_c
                part_max = s_c[:, 0:LANES]
                for pt in range(1, cw // LANES):
                    part_max = jnp.maximum(part_max, s_c[:, pt * LANES:(pt + 1) * LANES])
                w_refs[m][...] = part_max if ci == 0 else jnp.maximum(w_refs[m][...], part_max)

    @pl.when(jnp.logical_or(before, after))
    def _():
        scores(False)

    @pl.when(jnp.logical_not(jnp.logical_or(before, after)))
    def _():
        scores(True)

    for m in range(2):
        s_ref, p_ref, wide_ref = s_refs[m], p_refs[m], w_refs[m]
        m_old = m_ref[m]
        m_new = jnp.maximum(m_old, jnp.max(wide_ref[...], axis=-1, keepdims=True) + row_off)
        alpha = jnp.exp2(m_old - m_new)
        m_ref[m] = m_new
        wide_ref[...] = jnp.broadcast_to(m_new - row_off, (tq, LANES))
        per_group = ATTN_PV_ROWS // rb
        for grp in range(tq // ATTN_PV_ROWS):
            for r in range(grp * per_group, (grp + 1) * per_group):
                rows = slice(r * rb, (r + 1) * rb)
                shift_b = wide_ref[rows, :]
                part_sum = None
                for pt in range(nparts):
                    lanes = slice(pt * LANES, (pt + 1) * LANES)
                    p = jnp.exp2(s_ref[rows, lanes] - shift_b)
                    p_ref[rows, lanes] = p.astype(BF16)
                    part_sum = p if part_sum is None else part_sum + p
                wide_ref[rows, :] = part_sum
            grows = slice(grp * ATTN_PV_ROWS, (grp + 1) * ATTN_PV_ROWS)
            acc_ref[m, grows, :] = alpha[grows] * acc_ref[m, grows, :] + _dot(p_ref[grows, :], v_ref[...])
        l_ref[m] = alpha * l_ref[m] + jnp.sum(wide_ref[...], axis=-1, keepdims=True)

    @pl.when(ki == pl.num_programs(3) - 1)
    def _():
        lf = lam_ref[...]
        lam_full = (jnp.exp(jnp.sum(lf[0:1] * lf[1:2], axis=-1, keepdims=True))
                    - jnp.exp(jnp.sum(lf[2:3] * lf[3:4], axis=-1, keepdims=True)) + lam_init)
        o = acc_ref[0] / l_ref[0] - lam_full * (acc_ref[1] / l_ref[1])
        o_ref[...] = (_rms(o, subln_ref[...]) * (1.0 - lam_init)).astype(o_ref.dtype)


def diff_attention_core(qkv, lam, subln, lam_init):
    b, s, d3 = qkv.shape
    d = d3 // 3
    hw = 2 * A_Q_HEAD_DIM
    nh = d // hw
    tq = _pick(s, (ATTN_Q_TILE, 512, 256))
    tk = _pick(s, (ATTN_K_TILE, 512, 256))
    slopes = LOG2_E * 2.0 ** (-8.0 * jnp.arange(1, nh + 1, dtype=F32) / nh)
    kern = functools.partial(_diff_attn_kernel, lam_init=lam_init, tq=tq, tk=tk)
    return pl.pallas_call(
        kern,
        grid=(b, nh, s // tq, s // tk),
        in_specs=[
            pl.BlockSpec(memory_space=pltpu.SMEM),
            pl.BlockSpec((4, A_Q_HEAD_DIM), lambda bi, h, qi, ki: (0, 0)),
            pl.BlockSpec((1, hw), lambda bi, h, qi, ki: (0, 0)),
            pl.BlockSpec((None, tq, hw), lambda bi, h, qi, ki: (bi, qi, h)),
            pl.BlockSpec((None, tk, hw), lambda bi, h, qi, ki: (bi, ki, nh + h)),
            pl.BlockSpec((None, tk, hw), lambda bi, h, qi, ki: (bi, ki, 2 * nh + h)),
        ],
        out_specs=pl.BlockSpec((None, tq, hw), lambda bi, h, qi, ki: (bi, qi, h)),
        out_shape=jax.ShapeDtypeStruct((b, s, d), BF16),
        scratch_shapes=[
            pltpu.VMEM((tq, tk), F32),
            pltpu.VMEM((tq, tk), F32),
            pltpu.VMEM((tq, tk), BF16),
            pltpu.VMEM((tq, tk), BF16),
            pltpu.VMEM((tq, LANES), F32),
            pltpu.VMEM((tq, LANES), F32),
            pltpu.VMEM((2, tq, 1), F32),
            pltpu.VMEM((2, tq, 1), F32),
            pltpu.VMEM((2, tq, hw), F32),
        ],
        compiler_params=_cparams("parallel", "parallel", "parallel", "arbitrary"),
        name="diff_attention_core",
    )(slopes, lam.astype(F32), subln.reshape(1, hw).astype(F32), qkv, qkv, qkv)


POOL_HALO = 16


def _pool_kernel(xp_ref, xc_ref, xn_ref, gin_ref, gout_ref, w_ref, scale_ref, o_ref, *, seq, tm):
    i = pl.program_id(1)
    n = pl.num_programs(1)
    gin = gin_ref[...]
    xc = xc_ref[...]
    uc = _rms(xc, gin)
    up = _rms(xp_ref[...], gin) * (i > 0).astype(F32)
    un = _rms(xn_ref[...], gin) * (i < n - 1).astype(F32)
    grp = w_ref.shape[1]
    hal = POOL_HALO

    r_c = lax.broadcasted_iota(jnp.int32, (tm, tm), 0)
    c_c = lax.broadcasted_iota(jnp.int32, (tm, tm), 1)
    d_c = c_c - r_c
    r_h = lax.broadcasted_iota(jnp.int32, (tm, hal), 0)
    c_h = lax.broadcasted_iota(jnp.int32, (tm, hal), 1)
    d_p = c_h - hal - r_h
    d_n = c_h + tm - r_h
    t_abs = i * tm + lax.broadcasted_iota(jnp.int32, (tm, 1), 0)

    outs = []
    for gi, win in enumerate(POOL_WINDOWS):
        half = win // 2
        sl = slice(gi * grp, (gi + 1) * grp)
        band_c = ((d_c >= -half) & (d_c < half)).astype(F32)
        band_p = (d_p >= -half).astype(F32)
        band_n = (d_n < half).astype(F32)
        wsum = (_dot_exact_lhs(band_c, uc[:, sl]) + _dot_exact_lhs(band_p, up[:, sl])
                + _dot_exact_lhs(band_n, un[:, sl]))
        lo = jnp.clip(t_abs - half, 0, seq)
        hi = jnp.clip(t_abs + half, 0, seq)
        count = (hi - lo).astype(F32)
        pooled = wsum / count - uc[:, sl]
        outs.append(_dot(pooled.astype(BF16), w_ref[gi]))
    y = jnp.concatenate(outs, axis=-1) * scale_ref[...]
    o_ref[...] = xc + _rms(y, gout_ref[...])


def pool_sublayer(x, g_in, g_out, w_groups, scale):
    b, s, d = x.shape
    tm = _pick(s, (256, 128, 64, 32, 16))
    hal = POOL_HALO
    nh = tm // hal
    last = s // hal - 1
    kern = functools.partial(_pool_kernel, seq=s, tm=tm)
    grp = w_groups.shape[1]
    return pl.pallas_call(
        kern,
        grid=(b, s // tm),
        in_specs=[
            pl.BlockSpec((None, hal, d), lambda bi, i: (bi, jnp.maximum(i * nh - 1, 0), 0)),
            pl.BlockSpec((None, tm, d), lambda bi, i: (bi, i, 0)),
            pl.BlockSpec((None, hal, d), lambda bi, i: (bi, jnp.minimum((i + 1) * nh, last), 0)),
            pl.BlockSpec((1, d), lambda bi, i: (0, 0)),
            pl.BlockSpec((1, d), lambda bi, i: (0, 0)),
            pl.BlockSpec((len(POOL_WINDOWS), grp, grp), lambda bi, i: (0, 0, 0)),
            pl.BlockSpec((1, d), lambda bi, i: (0, 0)),
        ],
        out_specs=pl.BlockSpec((None, tm, d), lambda bi, i: (bi, i, 0)),
        out_shape=jax.ShapeDtypeStruct((b, s, d), F32),
        compiler_params=_cparams("parallel", "parallel"),
        name="pool_sublayer",
    )(x, x, x, g_in.reshape(1, d), g_out.reshape(1, d), w_groups, scale.reshape(1, d).astype(F32))


CONV_HALO = 16


def _gdn_conv_kernel(xp_ref, xc_ref, xn_ref, w_ref, o_ref, ext_ref, *, tb, n_q_blocks, n_qk_blocks):
    i = pl.program_id(1)
    j = pl.program_id(2)
    n = pl.num_programs(1)
    hal = CONV_HALO
    left = GDN_CONV // 2
    ext_ref[0:hal, :] = xp_ref[...].astype(F32) * (i > 0).astype(F32)
    ext_ref[hal:hal + tb, :] = xc_ref[...].astype(F32)
    ext_ref[hal + tb:hal + tb + hal, :] = xn_ref[...].astype(F32) * (i < n - 1).astype(F32)
    acc = None
    for kk in range(GDN_CONV):
        start = hal - left + kk
        term = ext_ref[start:start + tb, :] * w_ref[kk:kk + 1, :]
        acc = term if acc is None else acc + term
    y = _silu(acc)

    @pl.when(j >= n_qk_blocks)
    def _():
        o_ref[...] = y.astype(o_ref.dtype)

    @pl.when(j < n_qk_blocks)
    def _():
        qscale = jnp.where(j < n_q_blocks, GDN_K_DIM ** -0.5, 1.0).astype(F32)
        for hh in range(y.shape[1] // GDN_K_DIM):
            sl = slice(hh * GDN_K_DIM, (hh + 1) * GDN_K_DIM)
            t = y[:, sl]
            nrm = lax.rsqrt(jnp.sum(t * t, axis=-1, keepdims=True) + RMS_EPS) * qscale
            o_ref[:, sl] = (t * nrm).astype(o_ref.dtype)


def gdn_conv(qkvz, conv_w, qk_w, conv_dim):
    b, s, _ = qkvz.shape
    tb = _pick(s, (256, 128, 64, 32, 16))
    cb = _pick(qk_w, (1024, 512, 256, 128))
    hal = CONV_HALO
    nh = tb // hal
    last = s // hal - 1
    kern = functools.partial(_gdn_conv_kernel, tb=tb, n_q_blocks=qk_w // cb, n_qk_blocks=2 * qk_w // cb)
    return pl.pallas_call(
        kern,
        grid=(b, s // tb, conv_dim // cb),
        in_specs=[
            pl.BlockSpec((None, hal, cb), lambda bi, i, j: (bi, jnp.maximum(i * nh - 1, 0), j)),
            pl.BlockSpec((None, tb, cb), lambda bi, i, j: (bi, i, j)),
            pl.BlockSpec((None, hal, cb), lambda bi, i, j: (bi, jnp.minimum((i + 1) * nh, last), j)),
            pl.BlockSpec((GDN_CONV, cb), lambda bi, i, j: (0, j)),
        ],
        out_specs=pl.BlockSpec((None, tb, cb), lambda bi, i, j: (bi, i, j)),
        out_shape=jax.ShapeDtypeStruct((b, s, conv_dim), BF16),
        scratch_shapes=[pltpu.VMEM((tb + 2 * hal, cb), F32)],
        compiler_params=_cparams("parallel", "parallel", "parallel"),
        name="gdn_conv",
    )(qkvz, qkvz, qkvz, conv_w.astype(F32))


GDN_HEAD_GROUP = 4


def _split2(x):
    hi = x.astype(BF16)
    lo = (x - hi.astype(F32)).astype(BF16)
    return hi, lo


def _dot_hi_pieces(a_pieces, b_pieces):
    ah, al = a_pieces
    bh, bl = b_pieces
    return _dot(ah, bh) + _dot(ah, bl) + _dot(al, bh)


def _gdn_core_kernel(q_ref, k_ref, v_ref, ab_ref, alog_ref, dt_ref, o_ref, s_ref, *,
                     direction, tb, n_vheads, hg):
    grp = pl.program_id(1)
    blk = pl.program_id(2)
    c = GDN_CHUNK
    nchunk = tb // c
    dk = GDN_K_DIM
    dv = GDN_V_DIM
    assert dk == 2 * c and dv == dk
    hv2 = 2 * n_vheads
    nvh = 2 * hg
    abw = ab_ref.shape[1]

    @pl.when(blk == 0)
    def _():
        s_ref[...] = jnp.zeros_like(s_ref)

    ab = ab_ref[...]
    lane = lax.broadcasted_iota(jnp.int32, ab.shape, 1)
    g_all = -jnp.exp(alog_ref[...]) * jax.nn.softplus(ab + dt_ref[...])
    beta_all = jax.nn.sigmoid(ab)

    r_t = lax.broadcasted_iota(jnp.int32, (tb, tb), 0)
    c_t = lax.broadcasted_iota(jnp.int32, (tb, tb), 1)
    same_chunk = _div_pow2(r_t, c) == _div_pow2(c_t, c)
    tri = (c_t <= r_t) if direction == 0 else (c_t >= r_t)
    cum_mat = (same_chunk & tri).astype(F32)
    gc_all = _dot_exact_lhs(cum_mat, jnp.where(lane < hv2, g_all, 0.0))

    col0 = direction * n_vheads + nvh * grp
    sel_r = lax.broadcasted_iota(jnp.int32, (abw, nvh * dv), 0)
    sel_c = lax.broadcasted_iota(jnp.int32, (abw, nvh * dv), 1)
    a_col = col0 + _div_pow2(sel_c, dv)
    gcb_all = _dot_exact_rhs(gc_all, (sel_r == a_col).astype(F32))
    bb_all = _dot_exact_rhs(beta_all, (sel_r == a_col + hv2).astype(F32))

    gc_dup = jnp.concatenate([gc_all[(i // 2) * c:(i // 2 + 1) * c] for i in range(2 * nchunk)], axis=0)
    nrow = max(16, nvh)
    rs_r = lax.broadcasted_iota(jnp.int32, (nrow, abw), 0)
    rs_c = lax.broadcasted_iota(jnp.int32, (nrow, abw), 1)
    rsel = (rs_c == col0 + rs_r).astype(BF16)
    d_hi, d_mid, d_lo = _split3(gc_dup)
    rows_g = _dot_nt(rsel, d_hi) + _dot_nt(rsel, d_mid) + _dot_nt(rsel, d_lo)

    ii = lax.broadcasted_iota(jnp.int32, (c, 2 * c), 0)
    ll = lax.broadcasted_iota(jnp.int32, (c, 2 * c), 1)
    jj = jnp.bitwise_and(ll, c - 1)
    second = ll >= c
    if direction == 0:
        m_incl, m_strict, last_row = ii >= jj, ii > jj, c - 1
    else:
        m_incl, m_strict, last_row = ii <= jj, ii < jj, 0
    eye_p = (ii == jj).astype(F32)
    bd_r = lax.broadcasted_iota(jnp.int32, (2 * c, 2 * c), 0)
    bd_c = lax.broadcasted_iota(jnp.int32, (2 * c, 2 * c), 1)
    bd_mask = (bd_r >= c) == (bd_c >= c)
    second_row = lax.broadcasted_iota(jnp.int32, (1, 2 * c), 1) >= c
    lane_sol = lax.broadcasted_iota(jnp.int32, (c, 4 * dv), 1)
    sol_second = jnp.bitwise_and(_div_pow2(lane_sol, dv), 1) == 1
    bd2_r = lax.broadcasted_iota(jnp.int32, (2 * c, 2 * dv), 0)
    bd2_c = lax.broadcasted_iota(jnp.int32, (2 * c, 2 * dv), 1)
    bd2_mask = (bd2_r >= c) == (bd2_c >= dv)
    st_r = lax.broadcasted_iota(jnp.int32, (2 * dk, 2 * dv), 0)
    st_c = lax.broadcasted_iota(jnp.int32, (2 * dk, 2 * dv), 1)
    st_mask = (st_r >= dk) == (st_c >= dv)
    st_second = lax.broadcasted_iota(jnp.int32, (2 * dk, dv), 0) >= dk

    def block_diag(piece):
        return jnp.where(bd_mask, jnp.concatenate([piece, piece], axis=0), 0)

    order = list(range(nchunk)) if direction == 0 else list(range(nchunk - 1, -1, -1))
    probs = [(ch, hh) for ch in order for hh in range(hg)]

    pre = {}
    for ch, hh in probs:
        rows = slice(ch * c, (ch + 1) * c)
        k = k_ref[rows, hh * dk:(hh + 1) * dk]
        q = q_ref[rows, hh * dk:(hh + 1) * dk]
        k2 = jnp.concatenate([k, k], axis=0)
        gram = _dot_nt(jnp.concatenate([k, q], axis=0), k2)
        kk_p = gram[:c]
        qk_p = gram[c:]
        g_full = gcb_all[rows, 2 * hh * dv:(2 * hh + 2) * dv]
        b_full = bb_all[rows, 2 * hh * dv:(2 * hh + 2) * dv]
        g_p = jnp.where(second, g_full[:, dv:dv + 2 * c], g_full[:, :2 * c])
        b_p = jnp.where(second, b_full[:, dv:dv + 2 * c], b_full[:, :2 * c])
        r_blk = slice(ch * 2 * c, (ch + 1) * 2 * c)
        r_row = jnp.where(second_row, rows_g[2 * hh + 1:2 * hh + 2, r_blk], rows_g[2 * hh:2 * hh + 1, r_blk])
        decay = jnp.exp(jnp.where(m_incl, g_p - r_row, NEG_BIG))
        a_p = jnp.where(m_strict, b_p * kk_p * decay, 0.0)
        qkm_p = jnp.where(m_incl, qk_p * decay, 0.0)
        kf2 = jnp.concatenate([k, k], axis=1).astype(F32)
        qf2 = jnp.concatenate([q, q], axis=1).astype(F32)
        eg = jnp.exp(g_full)
        vb = v_ref[rows, hh * 2 * dv:(hh + 1) * 2 * dv].astype(F32) * b_full
        kb = kf2 * (b_full * eg)
        x = jnp.concatenate([vb, kb], axis=1)
        rhs = jnp.concatenate([jnp.where(sol_second, 0.0, x), jnp.where(sol_second, x, 0.0)],
                              axis=0).astype(BF16)
        g_last = g_full[last_row:last_row + 1, :]
        pre[(ch, hh)] = dict(mp=-a_p, t=eye_p - a_p, qkm=qkm_p.astype(BF16), rhs=rhs,
                             qs=(qf2 * eg).astype(BF16), kd=(kf2 * jnp.exp(g_last - g_full)).astype(BF16),
                             egl=jnp.exp(g_last))

    def split_and_diag(p):
        p['mpc'] = _split2(p['mp'])
        p['mbd'] = tuple(block_diag(x) for x in p['mpc'])

    for key in probs:
        split_and_diag(pre[key])
    for key in probs:
        p = pre[key]
        p['mp'] = _dot_hi_pieces(p['mpc'], p['mbd'])
    n_levels = int(math.log2(c)) - 1
    for lvl in range(n_levels):
        last = lvl == n_levels - 1
        for key in probs:
            split_and_diag(pre[key])
        for key in probs:
            p = pre[key]
            mh, ml = p['mpc']
            bh, bl = p['mbd']
            th, tl = _split2(p['t'])
            if last:
                p['t'] = p['t'] + _dot_hi_pieces((th, tl), (bh, bl))
            else:
                by_hi = _dot(jnp.concatenate([mh, ml, th, tl], axis=0), bh)
                by_lo = _dot(jnp.concatenate([mh, th], axis=0), bl)
                p['mp'] = by_hi[:c] + by_hi[c:2 * c] + by_lo[:c]
                p['t'] = p['t'] + (by_hi[2 * c:3 * c] + by_hi[3 * c:] + by_lo[c:])
    for key in probs:
        p = pre[key]
        sol = _dot(p['t'].astype(BF16), p['rhs'])
        p['u'] = sol[:, :2 * dv]
        p['w'] = sol[:, 2 * dv:].astype(BF16)

    for ch in order:
        rows = slice(ch * c, (ch + 1) * c)
        states = [s_ref[hh] for hh in range(hg)]
        sbs = [jnp.where(st_mask, jnp.concatenate([s, s], axis=1), 0.0).astype(BF16) for s in states]
        wq_s = [_dot(jnp.concatenate([pre[(ch, hh)]['w'], pre[(ch, hh)]['qs']], axis=0), sbs[hh])
                for hh in range(hg)]
        for hh in range(hg):
            p = pre[(ch, hh)]
            vnb = (p['u'] - wq_s[hh][:c]).astype(BF16)
            vn2 = jnp.where(bd2_mask, jnp.concatenate([vnb, vnb], axis=0), 0)
            o = wq_s[hh][c:] + _dot(p['qkm'], vn2)
            o_ref[rows, hh * 2 * dv:(hh + 1) * 2 * dv] = o.astype(o_ref.dtype)
            upd = _dot_tn(p['kd'], vnb)
            upd_d = jnp.where(st_second, upd[:, dv:], upd[:, :dv])
            egl = p['egl']
            dl = jnp.where(st_second, jnp.broadcast_to(egl[:, dv:], (2 * dk, dv)),
                           jnp.broadcast_to(egl[:, :dv], (2 * dk, dv)))
            s_ref[hh] = states[hh] * dl + upd_d


def gdn_core(qkv, ab, a_log, dt_bias, direction, n_kheads, n_vheads):
    b, s, _ = qkv.shape
    tb = _pick(s, (256, 128, 64))
    nblk = s // tb
    abw = ab.shape[-1]
    hg = _pick(n_kheads, (GDN_HEAD_GROUP, 2, 1))
    ngrp = n_kheads // hg
    pad = jnp.zeros((2 * n_vheads,), F32)
    alog_row = jnp.concatenate([a_log.reshape(-1).astype(F32), pad]).reshape(1, abw)
    dt_row = jnp.concatenate([dt_bias.reshape(-1).astype(F32), pad]).reshape(1, abw)

    def tok(blk):
        return blk if direction == 0 else nblk - 1 - blk

    kern = functools.partial(_gdn_core_kernel, direction=direction, tb=tb, n_vheads=n_vheads, hg=hg)
    qw = hg * GDN_K_DIM
    vw = hg * 2 * GDN_V_DIM
    return pl.pallas_call(
        kern,
        grid=(b, ngrp, nblk),
        in_specs=[
            pl.BlockSpec((None, tb, qw), lambda bi, g, blk: (bi, tok(blk), g)),
            pl.BlockSpec((None, tb, qw), lambda bi, g, blk: (bi, tok(blk), ngrp + g)),
            pl.BlockSpec((None, tb, vw), lambda bi, g, blk: (bi, tok(blk), ngrp + g)),
            pl.BlockSpec((None, tb, abw), lambda bi, g, blk: (bi, tok(blk), 0)),
            pl.BlockSpec((1, abw), lambda bi, g, blk: (0, 0)),
            pl.BlockSpec((1, abw), lambda bi, g, blk: (0, 0)),
        ],
        out_specs=pl.BlockSpec((None, tb, vw), lambda bi, g, blk: (bi, tok(blk), g)),
        out_shape=jax.ShapeDtypeStruct((b, s, n_vheads * GDN_V_DIM), F32),
        scratch_shapes=[pltpu.VMEM((hg, 2 * GDN_K_DIM, GDN_V_DIM), F32)],
        compiler_params=_cparams("parallel", "parallel", "arbitrary"),
        name="gdn_core_fwd" if direction == 0 else "gdn_core_bwd",
    )(qkv, qkv, qkv, ab, alog_row, dt_row)


def _gdn_gate_kernel(of_ref, ob_ref, z_ref, g_ref, o_ref):
    g = g_ref[...]
    for hh in range(of_ref.shape[1] // GDN_V_DIM):
        sl = slice(hh * GDN_V_DIM, (hh + 1) * GDN_V_DIM)
        o = of_ref[:, sl] + ob_ref[:, sl]
        z = z_ref[:, sl].astype(F32)
        o_ref[:, sl] = (_rms(o, g) * _silu(z)).astype(o_ref.dtype)


def gdn_gate(o_fwd, o_bwd, qkvz, norm_g, z_col0):
    t, w = o_fwd.shape
    tm = _pick(t, (512, 256, 128, 64, 32, 16))
    cb = _pick(w, (1024, 512, 256, 128))
    zoff = z_col0 // cb
    return pl.pallas_call(
        _gdn_gate_kernel,
        grid=(t // tm, w // cb),
        in_specs=[
            pl.BlockSpec((tm, cb), lambda i, j: (i, j)),
            pl.BlockSpec((tm, cb), lambda i, j: (i, j)),
            pl.BlockSpec((tm, cb), lambda i, j: (i, zoff + j)),
            pl.BlockSpec((1, GDN_V_DIM), lambda i, j: (0, 0)),
        ],
        out_specs=pl.BlockSpec((tm, cb), lambda i, j: (i, j)),
        out_shape=jax.ShapeDtypeStruct((t, w), BF16),
        compiler_params=_cparams("parallel", "parallel"),
        name="gdn_gate",
    )(o_fwd, o_bwd, qkvz, norm_g.reshape(1, GDN_V_DIM).astype(F32))


NA_KEY_ROWS = 4
NA_KEY_BLOCKS = 4
NA_BAND_ROWS = NA_KEY_ROWS * NA_KEY_BLOCKS
NA_BAND_LEAD = NA_WIN_ROWS // 2


def _na_kernel(bias_ref, q_ref, *refs, grid_rows):
    k_refs = refs[:NA_KEY_BLOCKS]
    v_refs = refs[NA_KEY_BLOCKS:2 * NA_KEY_BLOCKS]
    o_ref = refs[2 * NA_KEY_BLOCKS]
    i = pl.program_id(2)
    rb = NA_ROW_BLOCK
    tq = rb * GRID_W
    tkb = NA_KEY_ROWS * GRID_W
    q = q_ref[...]
    s = jnp.concatenate([_dot_nt(q, k_ref[...]) for k_ref in k_refs], axis=-1)
    q_row = i * rb + _div_pow2(lax.broadcasted_iota(jnp.int32, (tq, 1), 0), GRID_W)
    win_lo = jnp.clip(q_row - NA_WIN_ROWS // 2, 0, grid_rows - NA_WIN_ROWS)
    k_row = (i * rb - NA_BAND_LEAD
             + _div_pow2(lax.broadcasted_iota(jnp.int32, (1, NA_KEY_BLOCKS * tkb), 1), GRID_W))
    valid = (k_row >= win_lo) & (k_row < win_lo + NA_WIN_ROWS)
    s = jnp.where(valid, s + bias_ref[...], NEG_BIG)
    m = jnp.max(s, axis=-1, keepdims=True)
    p = jnp.exp(s - m)
    l = jnp.sum(p, axis=-1, keepdims=True)
    pb = p.astype(BF16)
    o = _dot(pb[:, :tkb], v_refs[0][...])
    for j in range(1, NA_KEY_BLOCKS):
        o = o + _dot(pb[:, j * tkb:(j + 1) * tkb], v_refs[j][...])
    o_ref[...] = (o / l).astype(o_ref.dtype)


def _na_dense_bias(rpb):
    h = rpb.shape[0]
    rb, w, kw, khw = NA_ROW_BLOCK, GRID_W, NA_WIN_COLS, NA_WIN_ROWS
    cols = jnp.arange(w)
    col_start = jnp.clip(cols - kw // 2, 0, w - kw)
    dc = cols[None, :] - cols[:, None]
    col_ok = (cols[None, :] >= col_start[:, None]) & (cols[None, :] < col_start[:, None] + kw)
    tiles = rpb[:, :, jnp.clip(dc + kw - 1, 0, 2 * kw - 2)]
    tiles = jnp.where(col_ok[None, None], tiles.astype(F32), NEG_BIG)
    tiles = jnp.concatenate([tiles, jnp.full((h, 1, w, w), NEG_BIG, F32)], axis=1)
    rq = jnp.arange(rb)[:, None]
    rk = jnp.arange(NA_BAND_ROWS)[None, :]
    dr = rk - NA_BAND_LEAD - rq
    idx = jnp.where(jnp.abs(dr) <= khw - 1, dr + khw - 1, 2 * khw - 1)
    dense = tiles[:, idx]
    dense = dense.transpose(0, 1, 3, 2, 4).reshape(h, rb * w, NA_BAND_ROWS * w)
    return dense


def na_core(qkv, rpb):
    b, s, d3 = qkv.shape
    d = d3 // 3
    dh = NA_HEAD_DIM
    nh = d // dh
    grid_rows = s // GRID_W
    tq = NA_ROW_BLOCK * GRID_W
    nblk = s // tq
    bias = _na_dense_bias(rpb)
    tkb = NA_KEY_ROWS * GRID_W
    n_kblk = s // tkb
    first = (NA_ROW_BLOCK // NA_KEY_ROWS, NA_BAND_LEAD // NA_KEY_ROWS)

    def band_spec(j, col0):
        return pl.BlockSpec(
            (None, tkb, dh),
            lambda bi, h, i: (bi, jnp.clip(first[0] * i - first[1] + j, 0, n_kblk - 1), col0 + h))

    kern = functools.partial(_na_kernel, grid_rows=grid_rows)
    return pl.pallas_call(
        kern,
        grid=(b, nh, nblk),
        in_specs=[
            pl.BlockSpec((None, tq, NA_BAND_ROWS * GRID_W), lambda bi, h, i: (h, 0, 0)),
            pl.BlockSpec((None, tq, dh), lambda bi, h, i: (bi, i, h)),
            *[band_spec(j, nh) for j in range(NA_KEY_BLOCKS)],
            *[band_spec(j, 2 * nh) for j in range(NA_KEY_BLOCKS)],
        ],
        out_specs=pl.BlockSpec((None, tq, dh), lambda bi, h, i: (bi, i, h)),
        out_shape=jax.ShapeDtypeStruct((b, s, d), BF16),
        compiler_params=_cparams("parallel", "parallel", "parallel"),
        name="na_core",
    )(bias, qkv, *([qkv] * (2 * NA_KEY_BLOCKS)))


def _lambda_init(layer_idx):
    return 0.8 - 0.6 * math.exp(-0.3 * layer_idx)


def _scaled_q_weight(w_qkv, scale):
    d = w_qkv.shape[1] // 3
    return jnp.concatenate([w_qkv[:, :d] * scale, w_qkv[:, d:]], axis=1).astype(BF16)


def _trunk(x, p):
    b, s, d = x.shape
    depth = p['norms'].shape[0]
    xt = x.reshape(b * s, d)
    for i in range(depth):
        g = p['norms'][i]
        xt = ffn_half_step(xt, g[0], g[1], p['ffn_w_in'][i][0], p['ffn_w_out'][i][0])
        kind, j = i % N_MIXERS, i // N_MIXERS
        if kind == 0:
            w_qkv = _scaled_q_weight(p['a_w_qkv'][j], LOG2_E * A_Q_HEAD_DIM ** -0.5)
            qkv = norm_matmul(xt, g[2], w_qkv, BF16).reshape(b, s, 3 * d)
            o = diff_attention_core(qkv, p['a_lambda'][j], p['a_subln'][j], _lambda_init(i))
            xt = matmul_norm_res(o.reshape(b * s, d), p['a_w_out'][j].astype(BF16), g[3], xt)
        elif kind == 1:
            xt = pool_sublayer(xt.reshape(b, s, d), g[2], g[3], p['pool_w'][j].astype(BF16),
                               p['pool_scale'][j]).reshape(b * s, d)
        elif kind == 2:
            n_vheads = p['gdn_a_log'].shape[-1]
            v_w = n_vheads * GDN_V_DIM
            w_in = p['gdn_w_in'][j]
            conv_dim = p['gdn_conv'].shape[-1]
            qk_w = (conv_dim - v_w) // 2
            n_kheads = qk_w // GDN_K_DIM
            qkvz = norm_matmul(xt, g[2], w_in[:, :conv_dim + v_w].astype(BF16), BF16)
            ab = norm_matmul(xt, g[2], w_in[:, conv_dim + v_w:].astype(BF16), F32)
            qkv = gdn_conv(qkvz.reshape(b, s, conv_dim + v_w), p['gdn_conv'][j], qk_w, conv_dim)
            ab3 = ab.reshape(b, s, 4 * n_vheads)
            o_f = gdn_core(qkv, ab3, p['gdn_a_log'][j], p['gdn_dt_bias'][j], 0, n_kheads, n_vheads)
            o_b = gdn_core(qkv, ab3, p['gdn_a_log'][j], p['gdn_dt_bias'][j], 1, n_kheads, n_vheads)
            gated = gdn_gate(o_f.reshape(b * s, v_w), o_b.reshape(b * s, v_w), qkvz,
                             p['gdn_norm'][j], conv_dim)
            xt = matmul_norm_res(gated, p['gdn_w_out'][j].astype(BF16), g[3], xt)
        else:
            w_qkv = _scaled_q_weight(p['na_w_qkv'][j], NA_HEAD_DIM ** -0.5)
            qkv = norm_matmul(xt, g[2], w_qkv, BF16).reshape(b, s, 3 * d)
            o = na_core(qkv, p['na_rpb'][j])
            xt = matmul_norm_res(o.reshape(b * s, d), p['na_w_out'][j].astype(BF16), g[3], xt)
        xt = ffn_half_step(xt, g[4], g[5], p['ffn_w_in'][i][1], p['ffn_w_out'][i][1])
    return xt.reshape(b, s, d)


def kernel(x_prompt, x_sample, norms, ffn_w_in, ffn_w_out, a_w_qkv, a_lambda, a_subln, a_w_out, pool_w, pool_scale, gdn_w_in, gdn_conv, gdn_a_log, gdn_dt_bias, gdn_norm, gdn_w_out, na_w_qkv, na_rpb, na_w_out):
    p = dict(norms=norms, ffn_w_in=ffn_w_in.astype(BF16), ffn_w_out=ffn_w_out.astype(BF16),
             a_w_qkv=a_w_qkv, a_lambda=a_lambda, a_subln=a_subln, a_w_out=a_w_out, pool_w=pool_w,
             pool_scale=pool_scale, gdn_w_in=gdn_w_in, gdn_conv=gdn_conv, gdn_a_log=gdn_a_log,
             gdn_dt_bias=gdn_dt_bias, gdn_norm=gdn_norm, gdn_w_out=gdn_w_out, na_w_qkv=na_w_qkv,
             na_rpb=na_rpb, na_w_out=na_w_out)
    return (_trunk(x_prompt, p), _trunk(x_sample, p))
```

```python
import functools
import math

import jax
import jax.numpy as jnp
from jax import lax
from jax.experimental import pallas as pl
from jax.experimental.pallas import tpu as pltpu

F32 = jnp.float32
BF16 = jnp.bfloat16

RMS_EPS = 1e-6
LOG2_E = math.log2(math.e)
N_MIXERS = 4
GRID_W = 64
A_Q_HEAD_DIM = 128
POOL_WINDOWS = (2, 4, 8, 16)
GDN_K_DIM = 128
GDN_V_DIM = 128
GDN_CONV = 4
GDN_CHUNK = 64
NA_HEAD_DIM = 128
NA_WIN_ROWS = 8
NA_WIN_COLS = 16
NA_ROW_BLOCK = 8
NEG_BIG = -1e30

VMEM_LIMIT_BYTES = 56 * 1024 * 1024


def _cparams(*sem):
    return pltpu.CompilerParams(dimension_semantics=sem, vmem_limit_bytes=VMEM_LIMIT_BYTES)


def _rms(x, g):
    ms = jnp.mean(x * x, axis=-1, keepdims=True)
    return x * lax.rsqrt(ms + RMS_EPS) * g


def _silu(x):
    return x * jax.nn.sigmoid(x)


def _dot(a, b):
    return jnp.dot(a, b, preferred_element_type=F32)


def _dot_nt(a, b):
    return lax.dot_general(a, b, (((1,), (1,)), ((), ())), preferred_element_type=F32)


def _dot_tn(a, b):
    return lax.dot_general(a, b, (((0,), (0,)), ((), ())), preferred_element_type=F32)


def _split3(x):
    hi = x.astype(BF16)
    r1 = x - hi.astype(F32)
    mid = r1.astype(BF16)
    lo = (r1 - mid.astype(F32)).astype(BF16)
    return hi, mid, lo


def _dot_exact_lhs(a_exact, x):
    hi, mid, lo = _split3(x)
    a = a_exact.astype(BF16)
    return _dot(a, hi) + _dot(a, mid) + _dot(a, lo)


def _dot_exact_rhs(x, b_exact):
    hi, mid, lo = _split3(x)
    b = b_exact.astype(BF16)
    return _dot(hi, b) + _dot(mid, b) + _dot(lo, b)


def _dot_hi(a, b):
    ah = a.astype(BF16)
    al = (a - ah.astype(F32)).astype(BF16)
    bh = b.astype(BF16)
    bl = (b - bh.astype(F32)).astype(BF16)
    return _dot(ah, bh) + _dot(ah, bl) + _dot(al, bh)


def _div_pow2(x, n):
    assert n & (n - 1) == 0
    return lax.shift_right_logical(x, int(math.log2(n)))


def _pick(n, prefs):
    for p in prefs:
        if n % p == 0:
            return p
    return n


def _ffn_kernel(x_ref, g0_ref, g1_ref, wg_ref, wu_ref, wo_ref, o_ref, xn_ref):
    j = pl.program_id(1)

    @pl.when(j == 0)
    def _():
        xn_ref[...] = _rms(x_ref[...], g0_ref[...]).astype(BF16)
        o_ref[...] = jnp.zeros_like(o_ref)

    xn = xn_ref[...]
    h = _dot(xn, wg_ref[...])
    u = _dot(xn, wu_ref[...])
    a = (_silu(h) * u).astype(BF16)
    o_ref[...] += _dot(a, wo_ref[...])

    @pl.when(j == pl.num_programs(1) - 1)
    def _():
        o_ref[...] = x_ref[...] + 0.5 * _rms(o_ref[...], g1_ref[...])


def ffn_half_step(x, g0, g1, w_in, w_out):
    t, d = x.shape
    f = w_out.shape[0]
    tm = _pick(t, (512, 256, 128, 64, 32, 16, 8))
    tf = _pick(f, (512, 256, 128))
    nf = f // tf
    return pl.pallas_call(
        _ffn_kernel,
        grid=(t // tm, nf),
        in_specs=[
            pl.BlockSpec((tm, d), lambda i, j: (i, 0)),
            pl.BlockSpec((1, d), lambda i, j: (0, 0)),
            pl.BlockSpec((1, d), lambda i, j: (0, 0)),
            pl.BlockSpec((d, tf), lambda i, j: (0, j)),
            pl.BlockSpec((d, tf), lambda i, j: (0, j + nf)),
            pl.BlockSpec((tf, d), lambda i, j: (j, 0)),
        ],
        out_specs=pl.BlockSpec((tm, d), lambda i, j: (i, 0)),
        out_shape=jax.ShapeDtypeStruct((t, d), F32),
        scratch_shapes=[pltpu.VMEM((tm, d), BF16)],
        compiler_params=_cparams("parallel", "arbitrary"),
        name="ffn_half_step",
    )(x, g0.reshape(1, d), g1.reshape(1, d), w_in, w_in, w_out)


def _norm_matmul_kernel(x_ref, g_ref, w_ref, o_ref, xn_ref):
    @pl.when(pl.program_id(1) == 0)
    def _():
        xn_ref[...] = _rms(x_ref[...], g_ref[...]).astype(BF16)

    o_ref[...] = _dot(xn_ref[...], w_ref[...]).astype(o_ref.dtype)


def norm_matmul(x, g, w, out_dtype):
    t, d = x.shape
    n = w.shape[1]
    tm = _pick(t, (512, 256, 128, 64, 32, 16, 8))
    tn = _pick(n, (1024, 512, 256, 128))
    return pl.pallas_call(
        _norm_matmul_kernel,
        grid=(t // tm, n // tn),
        in_specs=[
            pl.BlockSpec((tm, d), lambda i, j: (i, 0)),
            pl.BlockSpec((1, d), lambda i, j: (0, 0)),
            pl.BlockSpec((d, tn), lambda i, j: (0, j)),
        ],
        out_specs=pl.BlockSpec((tm, tn), lambda i, j: (i, j)),
        out_shape=jax.ShapeDtypeStruct((t, n), out_dtype),
        scratch_shapes=[pltpu.VMEM((tm, d), BF16)],
        compiler_params=_cparams("parallel", "arbitrary"),
        name="norm_matmul",
    )(x, g.reshape(1, d), w)


def _matmul_norm_res_kernel(y_ref, w_ref, g_ref, x_ref, o_ref):
    k = pl.program_id(1)

    @pl.when(k == 0)
    def _():
        o_ref[...] = jnp.zeros_like(o_ref)

    o_ref[...] += _dot(y_ref[...], w_ref[...])

    @pl.when(k == pl.num_programs(1) - 1)
    def _():
        o_ref[...] = x_ref[...] + _rms(o_ref[...], g_ref[...])


def matmul_norm_res(y, w, g, x):
    t, kdim = y.shape
    d = w.shape[1]
    tm = _pick(t, (512, 256, 128, 64, 32, 16, 8))
    tk = _pick(kdim, (1024, 512, 256, 128))
    return pl.pallas_call(
        _matmul_norm_res_kernel,
        grid=(t // tm, kdim // tk),
        in_specs=[
            pl.BlockSpec((tm, tk), lambda i, k: (i, k)),
            pl.BlockSpec((tk, d), lambda i, k: (k, 0)),
            pl.BlockSpec((1, d), lambda i, k: (0, 0)),
            pl.BlockSpec((tm, d), lambda i, k: (i, 0)),
        ],
        out_specs=pl.BlockSpec((tm, d), lambda i, k: (i, 0)),
        out_shape=jax.ShapeDtypeStruct((t, d), F32),
        compiler_params=_cparams("parallel", "arbitrary"),
        name="matmul_norm_res",
    )(y, w, g.reshape(1, d), x)


ATTN_TILE = 1024
ATTN_COL_CHUNK = 256
LANES = 128


ATTN_ROW_BLOCK = 64
ATTN_PV_ROWS = 256


ATTN_ZERO_MARGIN = 160.0


def _attn_all_zero(slopes_ref, qn_ref, kn_ref, bi, h, qi, ki, nq, nk, nh, tq, tk):
    assert tq == tk
    gap = jnp.maximum(jnp.maximum(ki * tk - (qi + 1) * tq + 1, qi * tq - (ki + 1) * tk + 1), 0)
    penalty = slopes_ref[h] * gap.astype(F32)
    zero = gap > 0
    for m in range(2):
        qm = qn_ref[((bi * nq + qi) * nh + h) * 2 + m]
        km = kn_ref[((bi * nk + ki) * nh + h) * 2 + m]
        kd = kn_ref[((bi * nk + qi) * nh + h) * 2 + m]
        zero = jnp.logical_and(zero, penalty > qm * (km + kd) + ATTN_ZERO_MARGIN)
    return zero


def _attn_norms_kernel(x_ref, o_ref):
    ngroups = x_ref.shape[1] // LANES
    lane = lax.broadcasted_iota(jnp.int32, (1, LANES), 1)
    out = jnp.zeros((1, LANES), F32)
    for g in range(ngroups):
        x = x_ref[:, g * LANES:(g + 1) * LANES].astype(F32)
        n2 = jnp.max(jnp.sum(x * x, axis=-1, keepdims=True), axis=0, keepdims=True)
        out = jnp.where(lane == g, jnp.sqrt(n2), out)
    o_ref[...] = out


def attn_block_norms(qkv, width, tile):
    b, s, _ = qkv.shape
    ngroups = width // LANES
    assert ngroups <= LANES
    out = pl.pallas_call(
        _attn_norms_kernel,
        grid=(b, s // tile),
        in_specs=[pl.BlockSpec((None, tile, width), lambda bi, i: (bi, i, 0))],
        out_specs=pl.BlockSpec((None, None, 1, LANES), lambda bi, i: (bi, i, 0, 0)),
        out_shape=jax.ShapeDtypeStruct((b, s // tile, 1, LANES), F32),
        compiler_params=_cparams("parallel", "parallel"),
        name="attn_block_norms",
    )(qkv)
    return out[:, :, 0, :ngroups]


def _diff_attn_kernel(slopes_ref, qn_ref, kn_ref, lam_ref, subln_ref, q_ref, k_ref, v_ref, o_ref,
                      s0_ref, s1_ref, p0_ref, p1_ref, w0_ref, w1_ref, m_ref, l_ref, acc_ref, *,
                      lam_init, tq, tk):
    s_refs, p_refs, w_refs = (s0_ref, s1_ref), (p0_ref, p1_ref), (w0_ref, w1_ref)
    h = pl.program_id(1)
    qi = pl.program_id(2)
    ki = pl.program_id(3)
    dh = A_Q_HEAD_DIM
    cw = ATTN_COL_CHUNK
    rb = ATTN_ROW_BLOCK
    slope = slopes_ref[h]
    nparts = tk // LANES
    il = lax.broadcasted_iota(jnp.int32, (tq, 1), 0).astype(F32)
    jl = lax.broadcasted_iota(jnp.int32, (1, tk), 1).astype(F32)

    @pl.when(ki == 0)
    def _():
        m_ref[...] = jnp.full_like(m_ref, NEG_BIG)
        l_ref[...] = jnp.zeros_like(l_ref)
        acc_ref[...] = jnp.zeros_like(acc_ref)

    delta = qi * tq - ki * tk
    before = delta >= tk
    after = delta <= -tq
    sigma = jnp.where(before, slope, jnp.where(after, -slope, 0.0))
    row_off = -sigma * (il + delta.astype(F32))

    def scores(overlapping):
        for m in range(2):
            q = q_ref[:, m * dh:(m + 1) * dh]
            for ci in range(tk // cw):
                cols = slice(ci * cw, (ci + 1) * cw)
                s_c = _dot_nt(q, k_ref[cols, m * dh:(m + 1) * dh])
                if overlapping:
                    s_c = s_c - slope * jnp.abs((il + delta.astype(F32)) - jl[:, cols])
                else:
                    s_c = s_c + sigma * jl[:, cols]
                s_refs[m][:, cols] = s_c
                part_max = s_c[:, 0:LANES]
                for pt in range(1, cw // LANES):
                    part_max = jnp.maximum(part_max, s_c[:, pt * LANES:(pt + 1) * LANES])
                w_refs[m][...] = part_max if ci == 0 else jnp.maximum(w_refs[m][...], part_max)

    def softmax_and_pv():
        for m in range(2):
            s_ref, p_ref, wide_ref = s_refs[m], p_refs[m], w_refs[m]
            m_old = m_ref[m]
            m_new = jnp.maximum(m_old, jnp.max(wide_ref[...], axis=-1, keepdims=True) + row_off)
            alpha = jnp.exp2(m_old - m_new)
            m_ref[m] = m_new
            wide_ref[...] = jnp.broadcast_to(m_new - row_off, (tq, LANES))
            per_group = ATTN_PV_ROWS // rb
            for grp in range(tq // ATTN_PV_ROWS):
                for r in range(grp * per_group, (grp + 1) * per_group):
                    rows = slice(r * rb, (r + 1) * rb)
                    shift_b = wide_ref[rows, :]
                    part_sum = None
                    for pt in range(nparts):
                        lanes = slice(pt * LANES, (pt + 1) * LANES)
                        p = jnp.exp2(s_ref[rows, lanes] - shift_b)
                        p_ref[rows, lanes] = p.astype(BF16)
                        part_sum = p if part_sum is None else part_sum + p
                    wide_ref[rows, :] = part_sum
                grows = slice(grp * ATTN_PV_ROWS, (grp + 1) * ATTN_PV_ROWS)
                acc_ref[m, grows, :] = (alpha[grows] * acc_ref[m, grows, :]
                                        + _dot(p_ref[grows, :], v_ref[...]))
            l_ref[m] = alpha * l_ref[m] + jnp.sum(wide_ref[...], axis=-1, keepdims=True)

    live = jnp.logical_not(_attn_all_zero(slopes_ref, qn_ref, kn_ref, pl.program_id(0), h, qi, ki,
                                          pl.num_programs(2), pl.num_programs(3), pl.num_programs(1),
                                          tq, tk))

    @pl.when(jnp.logical_and(live, jnp.logical_or(before, after)))
    def _():
        scores(False)

    @pl.when(jnp.logical_and(live, jnp.logical_not(jnp.logical_or(before, after))))
    def _():
        scores(True)

    @pl.when(live)
    def _():
        softmax_and_pv()

    @pl.when(ki == pl.num_programs(3) - 1)
    def _():
        lf = lam_ref[...]
        lam_full = (jnp.exp(jnp.sum(lf[0:1] * lf[1:2], axis=-1, keepdims=True))
                    - jnp.exp(jnp.sum(lf[2:3] * lf[3:4], axis=-1, keepdims=True)) + lam_init)
        o = acc_ref[0] / l_ref[0] - lam_full * (acc_ref[1] / l_ref[1])
        o_ref[...] = (_rms(o, subln_ref[...]) * (1.0 - lam_init)).astype(o_ref.dtype)


def diff_attention_core(qkv, lam, subln, lam_init):
    b, s, d3 = qkv.shape
    d = d3 // 3
    hw = 2 * A_Q_HEAD_DIM
    nh = d // hw
    tq = _pick(s, (ATTN_TILE, 512, 256))
    tk = tq
    nq = s // tq
    slopes = LOG2_E * 2.0 ** (-8.0 * jnp.arange(1, nh + 1, dtype=F32) / nh)
    norms = attn_block_norms(qkv, 2 * d, tq)
    qn = norms[:, :, :2 * nh].reshape(-1)
    kn = norms[:, :, 2 * nh:].reshape(-1)
    kern = functools.partial(_diff_attn_kernel, lam_init=lam_init, tq=tq, tk=tk)

    def key_block(bi, h, qi, ki, slopes_ref, qn_ref, kn_ref):
        dead = _attn_all_zero(slopes_ref, qn_ref, kn_ref, bi, h, qi, ki, nq, nq, nh, tq, tk)
        return jnp.where(dead, qi, ki)

    grid_spec = pltpu.PrefetchScalarGridSpec(
        num_scalar_prefetch=3,
        grid=(b, nh, nq, nq),
        in_specs=[
            pl.BlockSpec((4, A_Q_HEAD_DIM), lambda bi, h, qi, ki, *_: (0, 0)),
            pl.BlockSpec((1, hw), lambda bi, h, qi, ki, *_: (0, 0)),
            pl.BlockSpec((None, tq, hw), lambda bi, h, qi, ki, *_: (bi, qi, h)),
            pl.BlockSpec((None, tk, hw), lambda bi, h, qi, ki, *pre: (bi, key_block(bi, h, qi, ki, *pre), nh + h)),
            pl.BlockSpec((None, tk, hw),
                         lambda bi, h, qi, ki, *pre: (bi, key_block(bi, h, qi, ki, *pre), 2 * nh + h)),
        ],
        out_specs=pl.BlockSpec((None, tq, hw), lambda bi, h, qi, ki, *_: (bi, qi, h)),
        scratch_shapes=[
            pltpu.VMEM((tq, tk), F32),
            pltpu.VMEM((tq, tk), F32),
            pltpu.VMEM((tq, tk), BF16),
            pltpu.VMEM((tq, tk), BF16),
            pltpu.VMEM((tq, LANES), F32),
            pltpu.VMEM((tq, LANES), F32),
            pltpu.VMEM((2, tq, 1), F32),
            pltpu.VMEM((2, tq, 1), F32),
            pltpu.VMEM((2, tq, hw), F32),
        ],
    )
    return pl.pallas_call(
        kern,
        grid_spec=grid_spec,
        out_shape=jax.ShapeDtypeStruct((b, s, d), BF16),
        compiler_params=_cparams("parallel", "parallel", "parallel", "arbitrary"),
        name="diff_attention_core",
    )(slopes, qn, kn, lam.astype(F32), subln.reshape(1, hw).astype(F32), qkv, qkv, qkv)


POOL_HALO = 16


def _pool_kernel(xp_ref, xc_ref, xn_ref, gin_ref, gout_ref, w_ref, scale_ref, o_ref, *, seq, tm):
    i = pl.program_id(1)
    n = pl.num_programs(1)
    gin = gin_ref[...]
    xc = xc_ref[...]
    uc = _rms(xc, gin)
    up = _rms(xp_ref[...], gin) * (i > 0).astype(F32)
    un = _rms(xn_ref[...], gin) * (i < n - 1).astype(F32)
    grp = w_ref.shape[1]
    hal = POOL_HALO

    r_c = lax.broadcasted_iota(jnp.int32, (tm, tm), 0)
    c_c = lax.broadcasted_iota(jnp.int32, (tm, tm), 1)
    d_c = c_c - r_c
    r_h = lax.broadcasted_iota(jnp.int32, (tm, hal), 0)
    c_h = lax.broadcasted_iota(jnp.int32, (tm, hal), 1)
    d_p = c_h - hal - r_h
    d_n = c_h + tm - r_h
    t_abs = i * tm + lax.broadcasted_iota(jnp.int32, (tm, 1), 0)

    outs = []
    for gi, win in enumerate(POOL_WINDOWS):
        half = win // 2
        sl = slice(gi * grp, (gi + 1) * grp)
        band_c = ((d_c >= -half) & (d_c < half)).astype(F32)
        band_p = (d_p >= -half).astype(F32)
        band_n = (d_n < half).astype(F32)
        wsum = (_dot_exact_lhs(band_c, uc[:, sl]) + _dot_exact_lhs(band_p, up[:, sl])
                + _dot_exact_lhs(band_n, un[:, sl]))
        lo = jnp.clip(t_abs - half, 0, seq)
        hi = jnp.clip(t_abs + half, 0, seq)
        count = (hi - lo).astype(F32)
        pooled = wsum / count - uc[:, sl]
        outs.append(_dot(pooled.astype(BF16), w_ref[gi]))
    y = jnp.concatenate(outs, axis=-1) * scale_ref[...]
    o_ref[...] = xc + _rms(y, gout_ref[...])


def pool_sublayer(x, g_in, g_out, w_groups, scale):
    b, s, d = x.shape
    tm = _pick(s, (256, 128, 64, 32, 16))
    hal = POOL_HALO
    nh = tm // hal
    last = s // hal - 1
    kern = functools.partial(_pool_kernel, seq=s, tm=tm)
    grp = w_groups.shape[1]
    return pl.pallas_call(
        kern,
        grid=(b, s // tm),
        in_specs=[
            pl.BlockSpec((None, hal, d), lambda bi, i: (bi, jnp.maximum(i * nh - 1, 0), 0)),
            pl.BlockSpec((None, tm, d), lambda bi, i: (bi, i, 0)),
            pl.BlockSpec((None, hal, d), lambda bi, i: (bi, jnp.minimum((i + 1) * nh, last), 0)),
            pl.BlockSpec((1, d), lambda bi, i: (0, 0)),
            pl.BlockSpec((1, d), lambda bi, i: (0, 0)),
            pl.BlockSpec((len(POOL_WINDOWS), grp, grp), lambda bi, i: (0, 0, 0)),
            pl.BlockSpec((1, d), lambda bi, i: (0, 0)),
        ],
        out_specs=pl.BlockSpec((None, tm, d), lambda bi, i: (bi, i, 0)),
        out_shape=jax.ShapeDtypeStruct((b, s, d), F32),
        compiler_params=_cparams("parallel", "parallel"),
        name="pool_sublayer",
    )(x, x, x, g_in.reshape(1, d), g_out.reshape(1, d), w_groups, scale.reshape(1, d).astype(F32))


CONV_HALO = 16


def _gdn_conv_kernel(xp_ref, xc_ref, xn_ref, w_ref, o_ref, ext_ref, *, tb, n_q_blocks, n_qk_blocks):
    i = pl.program_id(1)
    j = pl.program_id(2)
    n = pl.num_programs(1)
    hal = CONV_HALO
    left = GDN_CONV // 2
    ext_ref[0:hal, :] = xp_ref[...].astype(F32) * (i > 0).astype(F32)
    ext_ref[hal:hal + tb, :] = xc_ref[...].astype(F32)
    ext_ref[hal + tb:hal + tb + hal, :] = xn_ref[...].astype(F32) * (i < n - 1).astype(F32)
    acc = None
    for kk in range(GDN_CONV):
        start = hal - left + kk
        term = ext_ref[start:start + tb, :] * w_ref[kk:kk + 1, :]
        acc = term if acc is None else acc + term
    y = _silu(acc)

    @pl.when(j >= n_qk_blocks)
    def _():
        o_ref[...] = y.astype(o_ref.dtype)

    @pl.when(j < n_qk_blocks)
    def _():
        qscale = jnp.where(j < n_q_blocks, GDN_K_DIM ** -0.5, 1.0).astype(F32)
        for hh in range(y.shape[1] // GDN_K_DIM):
            sl = slice(hh * GDN_K_DIM, (hh + 1) * GDN_K_DIM)
            t = y[:, sl]
            nrm = lax.rsqrt(jnp.sum(t * t, axis=-1, keepdims=True) + RMS_EPS) * qscale
            o_ref[:, sl] = (t * nrm).astype(o_ref.dtype)


def gdn_conv(qkvz, conv_w, qk_w, conv_dim):
    b, s, _ = qkvz.shape
    tb = _pick(s, (256, 128, 64, 32, 16))
    cb = _pick(qk_w, (1024, 512, 256, 128))
    hal = CONV_HALO
    nh = tb // hal
    last = s // hal - 1
    kern = functools.partial(_gdn_conv_kernel, tb=tb, n_q_blocks=qk_w // cb, n_qk_blocks=2 * qk_w // cb)
    return pl.pallas_call(
        kern,
        grid=(b, s // tb, conv_dim // cb),
        in_specs=[
            pl.BlockSpec((None, hal, cb), lambda bi, i, j: (bi, jnp.maximum(i * nh - 1, 0), j)),
            pl.BlockSpec((None, tb, cb), lambda bi, i, j: (bi, i, j)),
            pl.BlockSpec((None, hal, cb), lambda bi, i, j: (bi, jnp.minimum((i + 1) * nh, last), j)),
            pl.BlockSpec((GDN_CONV, cb), lambda bi, i, j: (0, j)),
        ],
        out_specs=pl.BlockSpec((None, tb, cb), lambda bi, i, j: (bi, i, j)),
        out_shape=jax.ShapeDtypeStruct((b, s, conv_dim), BF16),
        scratch_shapes=[pltpu.VMEM((tb + 2 * hal, cb), F32)],
        compiler_params=_cparams("parallel", "parallel", "parallel"),
        name="gdn_conv",
    )(qkvz, qkvz, qkvz, conv_w.astype(F32))


GDN_HEAD_GROUP = 4


def _split2(x):
    hi = x.astype(BF16)
    lo = (x - hi.astype(F32)).astype(BF16)
    return hi, lo


def _dot_hi_pieces(a_pieces, b_pieces):
    ah, al = a_pieces
    bh, bl = b_pieces
    return _dot(ah, bh) + _dot(ah, bl) + _dot(al, bh)


def _gdn_core_kernel(q_ref, k_ref, v_ref, ab_ref, alog_ref, dt_ref, o_ref, s_ref, *,
                     direction, tb, n_vheads, hg):
    grp = pl.program_id(1)
    blk = pl.program_id(2)
    c = GDN_CHUNK
    nchunk = tb // c
    dk = GDN_K_DIM
    dv = GDN_V_DIM
    assert dk == 2 * c and dv == dk
    hv2 = 2 * n_vheads
    nvh = 2 * hg
    abw = ab_ref.shape[1]

    @pl.when(blk == 0)
    def _():
        s_ref[...] = jnp.zeros_like(s_ref)

    ab = ab_ref[...]
    lane = lax.broadcasted_iota(jnp.int32, ab.shape, 1)
    g_all = -jnp.exp(alog_ref[...]) * jax.nn.softplus(ab + dt_ref[...])
    beta_all = jax.nn.sigmoid(ab)

    r_t = lax.broadcasted_iota(jnp.int32, (tb, tb), 0)
    c_t = lax.broadcasted_iota(jnp.int32, (tb, tb), 1)
    same_chunk = _div_pow2(r_t, c) == _div_pow2(c_t, c)
    tri = (c_t <= r_t) if direction == 0 else (c_t >= r_t)
    cum_mat = (same_chunk & tri).astype(F32)
    gc_all = _dot_exact_lhs(cum_mat, jnp.where(lane < hv2, g_all, 0.0))

    col0 = direction * n_vheads + nvh * grp
    sel_r = lax.broadcasted_iota(jnp.int32, (abw, nvh * dv), 0)
    sel_c = lax.broadcasted_iota(jnp.int32, (abw, nvh * dv), 1)
    a_col = col0 + _div_pow2(sel_c, dv)
    gcb_all = _dot_exact_rhs(gc_all, (sel_r == a_col).astype(F32))
    bb_all = _dot_exact_rhs(beta_all, (sel_r == a_col + hv2).astype(F32))

    gc_dup = jnp.concatenate([gc_all[(i // 2) * c:(i // 2 + 1) * c] for i in range(2 * nchunk)], axis=0)
    nrow = max(16, nvh)
    rs_r = lax.broadcasted_iota(jnp.int32, (nrow, abw), 0)
    rs_c = lax.broadcasted_iota(jnp.int32, (nrow, abw), 1)
    rsel = (rs_c == col0 + rs_r).astype(BF16)
    d_hi, d_mid, d_lo = _split3(gc_dup)
    rows_g = _dot_nt(rsel, d_hi) + _dot_nt(rsel, d_mid) + _dot_nt(rsel, d_lo)

    ii = lax.broadcasted_iota(jnp.int32, (c, 2 * c), 0)
    ll = lax.broadcasted_iota(jnp.int32, (c, 2 * c), 1)
    jj = jnp.bitwise_and(ll, c - 1)
    second = ll >= c
    if direction == 0:
        m_incl, m_strict, last_row = ii >= jj, ii > jj, c - 1
    else:
        m_incl, m_strict, last_row = ii <= jj, ii < jj, 0
    eye_p = (ii == jj).astype(F32)
    bd_r = lax.broadcasted_iota(jnp.int32, (2 * c, 2 * c), 0)
    bd_c = lax.broadcasted_iota(jnp.int32, (2 * c, 2 * c), 1)
    bd_mask = (bd_r >= c) == (bd_c >= c)
    second_row = lax.broadcasted_iota(jnp.int32, (1, 2 * c), 1) >= c
    lane_sol = lax.broadcasted_iota(jnp.int32, (c, 4 * dv), 1)
    sol_second = jnp.bitwise_and(_div_pow2(lane_sol, dv), 1) == 1
    bd2_r = lax.broadcasted_iota(jnp.int32, (2 * c, 2 * dv), 0)
    bd2_c = lax.broadcasted_iota(jnp.int32, (2 * c, 2 * dv), 1)
    bd2_mask = (bd2_r >= c) == (bd2_c >= dv)
    st_r = lax.broadcasted_iota(jnp.int32, (2 * dk, 2 * dv), 0)
    st_c = lax.broadcasted_iota(jnp.int32, (2 * dk, 2 * dv), 1)
    st_mask = (st_r >= dk) == (st_c >= dv)
    st_second = lax.broadcasted_iota(jnp.int32, (2 * dk, dv), 0) >= dk

    def block_diag(piece):
        return jnp.where(bd_mask, jnp.concatenate([piece, piece], axis=0), 0)

    order = list(range(nchunk)) if direction == 0 else list(range(nchunk - 1, -1, -1))
    probs = [(ch, hh) for ch in order for hh in range(hg)]

    pre = {}
    for ch, hh in probs:
        rows = slice(ch * c, (ch + 1) * c)
        k = k_ref[rows, hh * dk:(hh + 1) * dk]
        q = q_ref[rows, hh * dk:(hh + 1) * dk]
        k2 = jnp.concatenate([k, k], axis=0)
        gram = _dot_nt(jnp.concatenate([k, q], axis=0), k2)
        kk_p = gram[:c]
        qk_p = gram[c:]
        g_full = gcb_all[rows, 2 * hh * dv:(2 * hh + 2) * dv]
        b_full = bb_all[rows, 2 * hh * dv:(2 * hh + 2) * dv]
        g_p = jnp.where(second, g_full[:, dv:dv + 2 * c], g_full[:, :2 * c])
        b_p = jnp.where(second, b_full[:, dv:dv + 2 * c], b_full[:, :2 * c])
        r_blk = slice(ch * 2 * c, (ch + 1) * 2 * c)
        r_row = jnp.where(second_row, rows_g[2 * hh + 1:2 * hh + 2, r_blk], rows_g[2 * hh:2 * hh + 1, r_blk])
        decay = jnp.exp(jnp.where(m_incl, g_p - r_row, NEG_BIG))
        a_p = jnp.where(m_strict, b_p * kk_p * decay, 0.0)
        qkm_p = jnp.where(m_incl, qk_p * decay, 0.0)
        kf2 = jnp.concatenate([k, k], axis=1).astype(F32)
        qf2 = jnp.concatenate([q, q], axis=1).astype(F32)
        eg = jnp.exp(g_full)
        vb = v_ref[rows, hh * 2 * dv:(hh + 1) * 2 * dv].astype(F32) * b_full
        kb = kf2 * (b_full * eg)
        x = jnp.concatenate([vb, kb], axis=1)
        rhs = jnp.concatenate([jnp.where(sol_second, 0.0, x), jnp.where(sol_second, x, 0.0)],
                              axis=0).astype(BF16)
        g_last = g_full[last_row:last_row + 1, :]
        pre[(ch, hh)] = dict(mp=-a_p, t=eye_p - a_p, qkm=qkm_p.astype(BF16), rhs=rhs,
                             qs=(qf2 * eg).astype(BF16), kd=(kf2 * jnp.exp(g_last - g_full)).astype(BF16),
                             egl=jnp.exp(g_last))

    def split_and_diag(p):
        p['mpc'] = _split2(p['mp'])
        p['mbd'] = tuple(block_diag(x) for x in p['mpc'])

    for key in probs:
        split_and_diag(pre[key])
    for key in probs:
        p = pre[key]
        p['mp'] = _dot_hi_pieces(p['mpc'], p['mbd'])
    n_levels = int(math.log2(c)) - 1
    for lvl in range(n_levels):
        last = lvl == n_levels - 1
        for key in probs:
            split_and_diag(pre[key])
        for key in probs:
            p = pre[key]
            mh, ml = p['mpc']
            bh, bl = p['mbd']
            th, tl = _split2(p['t'])
            if last:
                p['t'] = p['t'] + _dot_hi_pieces((th, tl), (bh, bl))
            else:
                by_hi = _dot(jnp.concatenate([mh, ml, th, tl], axis=0), bh)
                by_lo = _dot(jnp.concatenate([mh, th], axis=0), bl)
                p['mp'] = by_hi[:c] + by_hi[c:2 * c] + by_lo[:c]
                p['t'] = p['t'] + (by_hi[2 * c:3 * c] + by_hi[3 * c:] + by_lo[c:])
    for key in probs:
        p = pre[key]
        sol = _dot(p['t'].astype(BF16), p['rhs'])
        p['u'] = sol[:, :2 * dv]
        p['w'] = sol[:, 2 * dv:].astype(BF16)

    for ch in order:
        rows = slice(ch * c, (ch + 1) * c)
        states = [s_ref[hh] for hh in range(hg)]
        sbs = [jnp.where(st_mask, jnp.concatenate([s, s], axis=1), 0.0).astype(BF16) for s in states]
        wq_s = [_dot(jnp.concatenate([pre[(ch, hh)]['w'], pre[(ch, hh)]['qs']], axis=0), sbs[hh])
                for hh in range(hg)]
        for hh in range(hg):
            p = pre[(ch, hh)]
            vnb = (p['u'] - wq_s[hh][:c]).astype(BF16)
            vn2 = jnp.where(bd2_mask, jnp.concatenate([vnb, vnb], axis=0), 0)
            o = wq_s[hh][c:] + _dot(p['qkm'], vn2)
            o_ref[rows, hh * 2 * dv:(hh + 1) * 2 * dv] = o.astype(o_ref.dtype)
            upd = _dot_tn(p['kd'], vnb)
            upd_d = jnp.where(st_second, upd[:, dv:], upd[:, :dv])
            egl = p['egl']
            dl = jnp.where(st_second, jnp.broadcast_to(egl[:, dv:], (2 * dk, dv)),
                           jnp.broadcast_to(egl[:, :dv], (2 * dk, dv)))
            s_ref[hh] = states[hh] * dl + upd_d


def gdn_core(qkv, ab, a_log, dt_bias, direction, n_kheads, n_vheads):
    b, s, _ = qkv.shape
    tb = _pick(s, (256, 128, 64))
    nblk = s // tb
    abw = ab.shape[-1]
    hg = _pick(n_kheads, (GDN_HEAD_GROUP, 2, 1))
    ngrp = n_kheads // hg
    pad = jnp.zeros((2 * n_vheads,), F32)
    alog_row = jnp.concatenate([a_log.reshape(-1).astype(F32), pad]).reshape(1, abw)
    dt_row = jnp.concatenate([dt_bias.reshape(-1).astype(F32), pad]).reshape(1, abw)

    def tok(blk):
        return blk if direction == 0 else nblk - 1 - blk

    kern = functools.partial(_gdn_core_kernel, direction=direction, tb=tb, n_vheads=n_vheads, hg=hg)
    qw = hg * GDN_K_DIM
    vw = hg * 2 * GDN_V_DIM
    return pl.pallas_call(
        kern,
        grid=(b, ngrp, nblk),
        in_specs=[
            pl.BlockSpec((None, tb, qw), lambda bi, g, blk: (bi, tok(blk), g)),
            pl.BlockSpec((None, tb, qw), lambda bi, g, blk: (bi, tok(blk), ngrp + g)),
            pl.BlockSpec((None, tb, vw), lambda bi, g, blk: (bi, tok(blk), ngrp + g)),
            pl.BlockSpec((None, tb, abw), lambda bi, g, blk: (bi, tok(blk), 0)),
            pl.BlockSpec((1, abw), lambda bi, g, blk: (0, 0)),
            pl.BlockSpec((1, abw), lambda bi, g, blk: (0, 0)),
        ],
        out_specs=pl.BlockSpec((None, tb, vw), lambda bi, g, blk: (bi, tok(blk), g)),
        out_shape=jax.ShapeDtypeStruct((b, s, n_vheads * GDN_V_DIM), F32),
        scratch_shapes=[pltpu.VMEM((hg, 2 * GDN_K_DIM, GDN_V_DIM), F32)],
        compiler_params=_cparams("parallel", "parallel", "arbitrary"),
        name="gdn_core_fwd" if direction == 0 else "gdn_core_bwd",
    )(qkv, qkv, qkv, ab, alog_row, dt_row)


def _gdn_gate_kernel(of_ref, ob_ref, z_ref, g_ref, o_ref):
    g = g_ref[...]
    for hh in range(of_ref.shape[1] // GDN_V_DIM):
        sl = slice(hh * GDN_V_DIM, (hh + 1) * GDN_V_DIM)
        o = of_ref[:, sl] + ob_ref[:, sl]
        z = z_ref[:, sl].astype(F32)
        o_ref[:, sl] = (_rms(o, g) * _silu(z)).astype(o_ref.dtype)


def gdn_gate(o_fwd, o_bwd, qkvz, norm_g, z_col0):
    t, w = o_fwd.shape
    tm = _pick(t, (512, 256, 128, 64, 32, 16))
    cb = _pick(w, (1024, 512, 256, 128))
    zoff = z_col0 // cb
    return pl.pallas_call(
        _gdn_gate_kernel,
        grid=(t // tm, w // cb),
        in_specs=[
            pl.BlockSpec((tm, cb), lambda i, j: (i, j)),
            pl.BlockSpec((tm, cb), lambda i, j: (i, j)),
            pl.BlockSpec((tm, cb), lambda i, j: (i, zoff + j)),
            pl.BlockSpec((1, GDN_V_DIM), lambda i, j: (0, 0)),
        ],
        out_specs=pl.BlockSpec((tm, cb), lambda i, j: (i, j)),
        out_shape=jax.ShapeDtypeStruct((t, w), BF16),
        compiler_params=_cparams("parallel", "parallel"),
        name="gdn_gate",
    )(o_fwd, o_bwd, qkvz, norm_g.reshape(1, GDN_V_DIM).astype(F32))


NA_KEY_ROWS = 4
NA_KEY_BLOCKS = 4
NA_BAND_ROWS = NA_KEY_ROWS * NA_KEY_BLOCKS
NA_BAND_LEAD = NA_WIN_ROWS // 2


def _na_kernel(bias_ref, q_ref, *refs, grid_rows):
    k_refs = refs[:NA_KEY_BLOCKS]
    v_refs = refs[NA_KEY_BLOCKS:2 * NA_KEY_BLOCKS]
    o_ref = refs[2 * NA_KEY_BLOCKS]
    i = pl.program_id(2)
    rb = NA_ROW_BLOCK
    tq = rb * GRID_W
    tkb = NA_KEY_ROWS * GRID_W
    q = q_ref[...]
    s = jnp.concatenate([_dot_nt(q, k_ref[...]) for k_ref in k_refs], axis=-1)
    q_row = i * rb + _div_pow2(lax.broadcasted_iota(jnp.int32, (tq, 1), 0), GRID_W)
    win_lo = jnp.clip(q_row - NA_WIN_ROWS // 2, 0, grid_rows - NA_WIN_ROWS)
    k_row = (i * rb - NA_BAND_LEAD
             + _div_pow2(lax.broadcasted_iota(jnp.int32, (1, NA_KEY_BLOCKS * tkb), 1), GRID_W))
    valid = (k_row >= win_lo) & (k_row < win_lo + NA_WIN_ROWS)
    s = jnp.where(valid, s + bias_ref[...], NEG_BIG)
    m = jnp.max(s, axis=-1, keepdims=True)
    p = jnp.exp(s - m)
    l = jnp.sum(p, axis=-1, keepdims=True)
    pb = p.astype(BF16)
    o = _dot(pb[:, :tkb], v_refs[0][...])
    for j in range(1, NA_KEY_BLOCKS):
        o = o + _dot(pb[:, j * tkb:(j + 1) * tkb], v_refs[j][...])
    o_ref[...] = (o / l).astype(o_ref.dtype)


def _na_dense_bias(rpb):
    h = rpb.shape[0]
    rb, w, kw, khw = NA_ROW_BLOCK, GRID_W, NA_WIN_COLS, NA_WIN_ROWS
    cols = jnp.arange(w)
    col_start = jnp.clip(cols - kw // 2, 0, w - kw)
    dc = cols[None, :] - cols[:, None]
    col_ok = (cols[None, :] >= col_start[:, None]) & (cols[None, :] < col_start[:, None] + kw)
    tiles = rpb[:, :, jnp.clip(dc + kw - 1, 0, 2 * kw - 2)]
    tiles = jnp.where(col_ok[None, None], tiles.astype(F32), NEG_BIG)
    tiles = jnp.concatenate([tiles, jnp.full((h, 1, w, w), NEG_BIG, F32)], axis=1)
    rq = jnp.arange(rb)[:, None]
    rk = jnp.arange(NA_BAND_ROWS)[None, :]
    dr = rk - NA_BAND_LEAD - rq
    idx = jnp.where(jnp.abs(dr) <= khw - 1, dr + khw - 1, 2 * khw - 1)
    dense = tiles[:, idx]
    dense = dense.transpose(0, 1, 3, 2, 4).reshape(h, rb * w, NA_BAND_ROWS * w)
    return dense


def na_core(qkv, rpb):
    b, s, d3 = qkv.shape
    d = d3 // 3
    dh = NA_HEAD_DIM
    nh = d // dh
    grid_rows = s // GRID_W
    tq = NA_ROW_BLOCK * GRID_W
    nblk = s // tq
    bias = _na_dense_bias(rpb)
    tkb = NA_KEY_ROWS * GRID_W
    n_kblk = s // tkb
    first = (NA_ROW_BLOCK // NA_KEY_ROWS, NA_BAND_LEAD // NA_KEY_ROWS)

    def band_spec(j, col0):
        return pl.BlockSpec(
            (None, tkb, dh),
            lambda bi, h, i: (bi, jnp.clip(first[0] * i - first[1] + j, 0, n_kblk - 1), col0 + h))

    kern = functools.partial(_na_kernel, grid_rows=grid_rows)
    return pl.pallas_call(
        kern,
        grid=(b, nh, nblk),
        in_specs=[
            pl.BlockSpec((None, tq, NA_BAND_ROWS * GRID_W), lambda bi, h, i: (h, 0, 0)),
            pl.BlockSpec((None, tq, dh), lambda bi, h, i: (bi, i, h)),
            *[band_spec(j, nh) for j in range(NA_KEY_BLOCKS)],
            *[band_spec(j, 2 * nh) for j in range(NA_KEY_BLOCKS)],
        ],
        out_specs=pl.BlockSpec((None, tq, dh), lambda bi, h, i: (bi, i, h)),
        out_shape=jax.ShapeDtypeStruct((b, s, d), BF16),
        compiler_params=_cparams("parallel", "parallel", "parallel"),
        name="na_core",
    )(bias, qkv, *([qkv] * (2 * NA_KEY_BLOCKS)))


def _lambda_init(layer_idx):
    return 0.8 - 0.6 * math.exp(-0.3 * layer_idx)


def _scaled_q_weight(w_qkv, scale):
    d = w_qkv.shape[1] // 3
    return jnp.concatenate([w_qkv[:, :d] * scale, w_qkv[:, d:]], axis=1).astype(BF16)


def _trunk(x, p):
    b, s, d = x.shape
    depth = p['norms'].shape[0]
    xt = x.reshape(b * s, d)
    for i in range(depth):
        g = p['norms'][i]
        xt = ffn_half_step(xt, g[0], g[1], p['ffn_w_in'][i][0], p['ffn_w_out'][i][0])
        kind, j = i % N_MIXERS, i // N_MIXERS
        if kind == 0:
            w_qkv = _scaled_q_weight(p['a_w_qkv'][j], LOG2_E * A_Q_HEAD_DIM ** -0.5)
            qkv = norm_matmul(xt, g[2], w_qkv, BF16).reshape(b, s, 3 * d)
            o = diff_attention_core(qkv, p['a_lambda'][j], p['a_subln'][j], _lambda_init(i))
            xt = matmul_norm_res(o.reshape(b * s, d), p['a_w_out'][j].astype(BF16), g[3], xt)
        elif kind == 1:
            xt = pool_sublayer(xt.reshape(b, s, d), g[2], g[3], p['pool_w'][j].astype(BF16),
                               p['pool_scale'][j]).reshape(b * s, d)
        elif kind == 2:
            n_vheads = p['gdn_a_log'].shape[-1]
            v_w = n_vheads * GDN_V_DIM
            w_in = p['gdn_w_in'][j]
            conv_dim = p['gdn_conv'].shape[-1]
            qk_w = (conv_dim - v_w) // 2
            n_kheads = qk_w // GDN_K_DIM
            qkvz = norm_matmul(xt, g[2], w_in[:, :conv_dim + v_w].astype(BF16), BF16)
            ab = norm_matmul(xt, g[2], w_in[:, conv_dim + v_w:].astype(BF16), F32)
            qkv = gdn_conv(qkvz.reshape(b, s, conv_dim + v_w), p['gdn_conv'][j], qk_w, conv_dim)
            ab3 = ab.reshape(b, s, 4 * n_vheads)
            o_f = gdn_core(qkv, ab3, p['gdn_a_log'][j], p['gdn_dt_bias'][j], 0, n_kheads, n_vheads)
            o_b = gdn_core(qkv, ab3, p['gdn_a_log'][j], p['gdn_dt_bias'][j], 1, n_kheads, n_vheads)
            gated = gdn_gate(o_f.reshape(b * s, v_w), o_b.reshape(b * s, v_w), qkvz,
                             p['gdn_norm'][j], conv_dim)
            xt = matmul_norm_res(gated, p['gdn_w_out'][j].astype(BF16), g[3], xt)
        else:
            w_qkv = _scaled_q_weight(p['na_w_qkv'][j], NA_HEAD_DIM ** -0.5)
            qkv = norm_matmul(xt, g[2], w_qkv, BF16).reshape(b, s, 3 * d)
            o = na_core(qkv, p['na_rpb'][j])
            xt = matmul_norm_res(o.reshape(b * s, d), p['na_w_out'][j].astype(BF16), g[3], xt)
        xt = ffn_half_step(xt, g[4], g[5], p['ffn_w_in'][i][1], p['ffn_w_out'][i][1])
    return xt.reshape(b, s, d)


def kernel(x_prompt, x_sample, norms, ffn_w_in, ffn_w_out, a_w_qkv, a_lambda, a_subln, a_w_out, pool_w, pool_scale, gdn_w_in, gdn_conv, gdn_a_log, gdn_dt_bias, gdn_norm, gdn_w_out, na_w_qkv, na_rpb, na_w_out):
    p = dict(norms=norms, ffn_w_in=ffn_w_in.astype(BF16), ffn_w_out=ffn_w_out.astype(BF16),
             a_w_qkv=a_w_qkv, a_lambda=a_lambda, a_subln=a_subln, a_w_out=a_w_out, pool_w=pool_w,
             pool_scale=pool_scale, gdn_w_in=gdn_w_in, gdn_conv=gdn_conv, gdn_a_log=gdn_a_log,
             gdn_dt_bias=gdn_dt_bias, gdn_norm=gdn_norm, gdn_w_out=gdn_w_out, na_w_qkv=na_w_qkv,
             na_rpb=na_rpb, na_w_out=na_w_out)
    return (_trunk(x_prompt, p), _trunk(x_sample, p))
```

```python
import functools
import math

import jax
import jax.numpy as jnp
from jax import lax
from jax.experimental import pallas as pl
from jax.experimental.pallas import tpu as pltpu

F32 = jnp.float32
BF16 = jnp.bfloat16

RMS_EPS = 1e-6
LOG2_E = math.log2(math.e)
N_MIXERS = 4
GRID_W = 64
A_Q_HEAD_DIM = 128
POOL_WINDOWS = (2, 4, 8, 16)
GDN_K_DIM = 128
GDN_V_DIM = 128
GDN_CONV = 4
GDN_CHUNK = 64
NA_HEAD_DIM = 128
NA_WIN_ROWS = 8
NA_WIN_COLS = 16
NA_ROW_BLOCK = 8
NEG_BIG = -1e30

VMEM_LIMIT_BYTES = 56 * 1024 * 1024


def _cparams(*sem):
    return pltpu.CompilerParams(dimension_semantics=sem, vmem_limit_bytes=VMEM_LIMIT_BYTES)


def _rms(x, g):
    ms = jnp.mean(x * x, axis=-1, keepdims=True)
    return x * lax.rsqrt(ms + RMS_EPS) * g


def _silu(x):
    return x * jax.nn.sigmoid(x)


def _dot(a, b):
    return jnp.dot(a, b, preferred_element_type=F32)


def _dot_nt(a, b):
    return lax.dot_general(a, b, (((1,), (1,)), ((), ())), preferred_element_type=F32)


def _dot_tn(a, b):
    return lax.dot_general(a, b, (((0,), (0,)), ((), ())), preferred_element_type=F32)


def _split3(x):
    hi = x.astype(BF16)
    r1 = x - hi.astype(F32)
    mid = r1.astype(BF16)
    lo = (r1 - mid.astype(F32)).astype(BF16)
    return hi, mid, lo


def _dot_exact_lhs(a_exact, x):
    hi, mid, lo = _split3(x)
    a = a_exact.astype(BF16)
    return _dot(a, hi) + _dot(a, mid) + _dot(a, lo)


def _dot_exact_rhs(x, b_exact):
    hi, mid, lo = _split3(x)
    b = b_exact.astype(BF16)
    return _dot(hi, b) + _dot(mid, b) + _dot(lo, b)


def _dot_hi(a, b):
    ah = a.astype(BF16)
    al = (a - ah.astype(F32)).astype(BF16)
    bh = b.astype(BF16)
    bl = (b - bh.astype(F32)).astype(BF16)
    return _dot(ah, bh) + _dot(ah, bl) + _dot(al, bh)


def _div_pow2(x, n):
    assert n & (n - 1) == 0
    return lax.shift_right_logical(x, int(math.log2(n)))


def _pick(n, prefs):
    for p in prefs:
        if n % p == 0:
            return p
    return n


def _ffn_kernel(x_ref, g0_ref, g1_ref, wg_ref, wu_ref, wo_ref, o_ref, xn_ref):
    j = pl.program_id(1)

    @pl.when(j == 0)
    def _():
        xn_ref[...] = _rms(x_ref[...], g0_ref[...]).astype(BF16)
        o_ref[...] = jnp.zeros_like(o_ref)

    xn = xn_ref[...]
    h = _dot(xn, wg_ref[...])
    u = _dot(xn, wu_ref[...])
    a = (_silu(h) * u).astype(BF16)
    o_ref[...] += _dot(a, wo_ref[...])

    @pl.when(j == pl.num_programs(1) - 1)
    def _():
        o_ref[...] = x_ref[...] + 0.5 * _rms(o_ref[...], g1_ref[...])


def ffn_half_step(x, g0, g1, w_in, w_out):
    t, d = x.shape
    f = w_out.shape[0]
    tm = _pick(t, (512, 256, 128, 64, 32, 16, 8))
    tf = _pick(f, (512, 256, 128))
    nf = f // tf
    return pl.pallas_call(
        _ffn_kernel,
        grid=(t // tm, nf),
        in_specs=[
            pl.BlockSpec((tm, d), lambda i, j: (i, 0)),
            pl.BlockSpec((1, d), lambda i, j: (0, 0)),
            pl.BlockSpec((1, d), lambda i, j: (0, 0)),
            pl.BlockSpec((d, tf), lambda i, j: (0, j)),
            pl.BlockSpec((d, tf), lambda i, j: (0, j + nf)),
            pl.BlockSpec((tf, d), lambda i, j: (j, 0)),
        ],
        out_specs=pl.BlockSpec((tm, d), lambda i, j: (i, 0)),
        out_shape=jax.ShapeDtypeStruct((t, d), F32),
        scratch_shapes=[pltpu.VMEM((tm, d), BF16)],
        compiler_params=_cparams("parallel", "arbitrary"),
        name="ffn_half_step",
    )(x, g0.reshape(1, d), g1.reshape(1, d), w_in, w_in, w_out)


def _norm_matmul_kernel(x_ref, g_ref, w_ref, o_ref, xn_ref):
    @pl.when(pl.program_id(1) == 0)
    def _():
        xn_ref[...] = _rms(x_ref[...], g_ref[...]).astype(BF16)

    o_ref[...] = _dot(xn_ref[...], w_ref[...]).astype(o_ref.dtype)


def norm_matmul(x, g, w, out_dtype):
    t, d = x.shape
    n = w.shape[1]
    tm = _pick(t, (1024, 512, 256, 128, 64, 32, 16, 8))
    tn = _pick(n, (1024, 512, 256, 128))
    return pl.pallas_call(
        _norm_matmul_kernel,
        grid=(t // tm, n // tn),
        in_specs=[
            pl.BlockSpec((tm, d), lambda i, j: (i, 0)),
            pl.BlockSpec((1, d), lambda i, j: (0, 0)),
            pl.BlockSpec((d, tn), lambda i, j: (0, j)),
        ],
        out_specs=pl.BlockSpec((tm, tn), lambda i, j: (i, j)),
        out_shape=jax.ShapeDtypeStruct((t, n), out_dtype),
        scratch_shapes=[pltpu.VMEM((tm, d), BF16)],
        compiler_params=_cparams("parallel", "arbitrary"),
        name="norm_matmul",
    )(x, g.reshape(1, d), w)


def _matmul_norm_res_kernel(y_ref, w_ref, g_ref, x_ref, o_ref):
    k = pl.program_id(1)

    @pl.when(k == 0)
    def _():
        o_ref[...] = jnp.zeros_like(o_ref)

    o_ref[...] += _dot(y_ref[...], w_ref[...])

    @pl.when(k == pl.num_programs(1) - 1)
    def _():
        o_ref[...] = x_ref[...] + _rms(o_ref[...], g_ref[...])


def matmul_norm_res(y, w, g, x):
    t, kdim = y.shape
    d = w.shape[1]
    tm = _pick(t, (512, 256, 128, 64, 32, 16, 8))
    tk = _pick(kdim, (2048, 1024, 512, 256, 128))
    return pl.pallas_call(
        _matmul_norm_res_kernel,
        grid=(t // tm, kdim // tk),
        in_specs=[
            pl.BlockSpec((tm, tk), lambda i, k: (i, k)),
            pl.BlockSpec((tk, d), lambda i, k: (k, 0)),
            pl.BlockSpec((1, d), lambda i, k: (0, 0)),
            pl.BlockSpec((tm, d), lambda i, k: (i, 0)),
        ],
        out_specs=pl.BlockSpec((tm, d), lambda i, k: (i, 0)),
        out_shape=jax.ShapeDtypeStruct((t, d), F32),
        compiler_params=_cparams("parallel", "arbitrary"),
        name="matmul_norm_res",
    )(y, w, g.reshape(1, d), x)


ATTN_TILE = 1024
ATTN_COL_CHUNK = 256
LANES = 128


ATTN_ROW_BLOCK = 64
ATTN_PV_ROWS = 256


ATTN_ZERO_MARGIN = 160.0


def _attn_all_zero(slopes_ref, qn_ref, kn_ref, bi, h, qi, ki, nq, nk, nh, tq, tk):
    assert tq == tk
    gap = jnp.maximum(jnp.maximum(ki * tk - (qi + 1) * tq + 1, qi * tq - (ki + 1) * tk + 1), 0)
    penalty = slopes_ref[h] * gap.astype(F32)
    zero = gap > 0
    for m in range(2):
        qm = qn_ref[((bi * nq + qi) * nh + h) * 2 + m]
        km = kn_ref[((bi * nk + ki) * nh + h) * 2 + m]
        kd = kn_ref[((bi * nk + qi) * nh + h) * 2 + m]
        zero = jnp.logical_and(zero, penalty > qm * (km + kd) + ATTN_ZERO_MARGIN)
    return zero


def _attn_norms_kernel(x_ref, o_ref):
    ngroups = x_ref.shape[1] // LANES
    lane = lax.broadcasted_iota(jnp.int32, (1, LANES), 1)
    out = jnp.zeros((1, LANES), F32)
    for g in range(ngroups):
        x = x_ref[:, g * LANES:(g + 1) * LANES].astype(F32)
        n2 = jnp.max(jnp.sum(x * x, axis=-1, keepdims=True), axis=0, keepdims=True)
        out = jnp.where(lane == g, jnp.sqrt(n2), out)
    o_ref[...] = out


def attn_block_norms(qkv, width, tile):
    b, s, _ = qkv.shape
    ngroups = width // LANES
    assert ngroups <= LANES
    out = pl.pallas_call(
        _attn_norms_kernel,
        grid=(b, s // tile),
        in_specs=[pl.BlockSpec((None, tile, width), lambda bi, i: (bi, i, 0))],
        out_specs=pl.BlockSpec((None, None, 1, LANES), lambda bi, i: (bi, i, 0, 0)),
        out_shape=jax.ShapeDtypeStruct((b, s // tile, 1, LANES), F32),
        compiler_params=_cparams("parallel", "parallel"),
        name="attn_block_norms",
    )(qkv)
    return out[:, :, 0, :ngroups]


def _diff_attn_kernel(slopes_ref, qn_ref, kn_ref, lam_ref, subln_ref, q_ref, k_ref, v_ref, o_ref,
                      s0_ref, s1_ref, p0_ref, p1_ref, w0_ref, w1_ref, m_ref, l_ref, acc_ref, *,
                      lam_init, tq, tk):
    s_refs, p_refs, w_refs = (s0_ref, s1_ref), (p0_ref, p1_ref), (w0_ref, w1_ref)
    h = pl.program_id(1)
    qi = pl.program_id(2)
    ki = pl.program_id(3)
    dh = A_Q_HEAD_DIM
    cw = ATTN_COL_CHUNK
    rb = ATTN_ROW_BLOCK
    slope = slopes_ref[h]
    nparts = tk // LANES

    @pl.when(ki == 0)
    def _():
        m_ref[...] = jnp.full_like(m_ref, NEG_BIG)
        l_ref[...] = jnp.zeros_like(l_ref)
        acc_ref[...] = jnp.zeros_like(acc_ref)

    delta = qi * tq - ki * tk
    before = delta >= tk
    after = delta <= -tq
    sigma = jnp.where(before, slope, jnp.where(after, -slope, 0.0))

    def query_offsets():
        return lax.broadcasted_iota(jnp.int32, (tq, 1), 0).astype(F32) + delta.astype(F32)

    def scores(overlapping):
        il = query_offsets()
        jl = lax.broadcasted_iota(jnp.int32, (1, tk), 1).astype(F32)
        for m in range(2):
            q = q_ref[:, m * dh:(m + 1) * dh]
            for ci in range(tk // cw):
                cols = slice(ci * cw, (ci + 1) * cw)
                s_c = _dot_nt(q, k_ref[cols, m * dh:(m + 1) * dh])
                if overlapping:
                    s_c = s_c - slope * jnp.abs(il - jl[:, cols])
                else:
                    s_c = s_c + sigma * jl[:, cols]
                s_refs[m][:, cols] = s_c
                part_max = s_c[:, 0:LANES]
                for pt in range(1, cw // LANES):
                    part_max = jnp.maximum(part_max, s_c[:, pt * LANES:(pt + 1) * LANES])
                w_refs[m][...] = part_max if ci == 0 else jnp.maximum(w_refs[m][...], part_max)

    def softmax_and_pv():
        row_off = -sigma * query_offsets()
        for m in range(2):
            s_ref, p_ref, wide_ref = s_refs[m], p_refs[m], w_refs[m]
            m_old = m_ref[m]
            m_new = jnp.maximum(m_old, jnp.max(wide_ref[...], axis=-1, keepdims=True) + row_off)
            alpha = jnp.exp2(m_old - m_new)
            m_ref[m] = m_new
            wide_ref[...] = jnp.broadcast_to(m_new - row_off, (tq, LANES))
            per_group = ATTN_PV_ROWS // rb
            for grp in range(tq // ATTN_PV_ROWS):
                for r in range(grp * per_group, (grp + 1) * per_group):
                    rows = slice(r * rb, (r + 1) * rb)
                    shift_b = wide_ref[rows, :]
                    part_sum = None
                    for pt in range(nparts):
                        lanes = slice(pt * LANES, (pt + 1) * LANES)
                        p = jnp.exp2(s_ref[rows, lanes] - shift_b)
                        p_ref[rows, lanes] = p.astype(BF16)
                        part_sum = p if part_sum is None else part_sum + p
                    wide_ref[rows, :] = part_sum
                grows = slice(grp * ATTN_PV_ROWS, (grp + 1) * ATTN_PV_ROWS)
                acc_ref[m, grows, :] = (alpha[grows] * acc_ref[m, grows, :]
                                        + _dot(p_ref[grows, :], v_ref[...]))
            l_ref[m] = alpha * l_ref[m] + jnp.sum(wide_ref[...], axis=-1, keepdims=True)

    live = jnp.logical_not(_attn_all_zero(slopes_ref, qn_ref, kn_ref, pl.program_id(0), h, qi, ki,
                                          pl.num_programs(2), pl.num_programs(3), pl.num_programs(1),
                                          tq, tk))

    @pl.when(jnp.logical_and(live, jnp.logical_or(before, after)))
    def _():
        scores(False)

    @pl.when(jnp.logical_and(live, jnp.logical_not(jnp.logical_or(before, after))))
    def _():
        scores(True)

    @pl.when(live)
    def _():
        softmax_and_pv()

    @pl.when(ki == pl.num_programs(3) - 1)
    def _():
        lf = lam_ref[...]
        lam_full = (jnp.exp(jnp.sum(lf[0:1] * lf[1:2], axis=-1, keepdims=True))
                    - jnp.exp(jnp.sum(lf[2:3] * lf[3:4], axis=-1, keepdims=True)) + lam_init)
        o = acc_ref[0] / l_ref[0] - lam_full * (acc_ref[1] / l_ref[1])
        o_ref[...] = (_rms(o, subln_ref[...]) * (1.0 - lam_init)).astype(o_ref.dtype)


def diff_attention_core(qkv, lam, subln, lam_init):
    b, s, d3 = qkv.shape
    d = d3 // 3
    hw = 2 * A_Q_HEAD_DIM
    nh = d // hw
    tq = _pick(s, (ATTN_TILE, 512, 256))
    tk = tq
    nq = s // tq
    slopes = LOG2_E * 2.0 ** (-8.0 * jnp.arange(1, nh + 1, dtype=F32) / nh)
    norms = attn_block_norms(qkv, 2 * d, tq)
    qn = norms[:, :, :2 * nh].reshape(-1)
    kn = norms[:, :, 2 * nh:].reshape(-1)
    kern = functools.partial(_diff_attn_kernel, lam_init=lam_init, tq=tq, tk=tk)

    def key_block(bi, h, qi, ki, slopes_ref, qn_ref, kn_ref):
        dead = _attn_all_zero(slopes_ref, qn_ref, kn_ref, bi, h, qi, ki, nq, nq, nh, tq, tk)
        return jnp.where(dead, qi, ki)

    grid_spec = pltpu.PrefetchScalarGridSpec(
        num_scalar_prefetch=3,
        grid=(b, nh, nq, nq),
        in_specs=[
            pl.BlockSpec((4, A_Q_HEAD_DIM), lambda bi, h, qi, ki, *_: (0, 0)),
            pl.BlockSpec((1, hw), lambda bi, h, qi, ki, *_: (0, 0)),
            pl.BlockSpec((None, tq, hw), lambda bi, h, qi, ki, *_: (bi, qi, h)),
            pl.BlockSpec((None, tk, hw), lambda bi, h, qi, ki, *pre: (bi, key_block(bi, h, qi, ki, *pre), nh + h)),
            pl.BlockSpec((None, tk, hw),
                         lambda bi, h, qi, ki, *pre: (bi, key_block(bi, h, qi, ki, *pre), 2 * nh + h)),
        ],
        out_specs=pl.BlockSpec((None, tq, hw), lambda bi, h, qi, ki, *_: (bi, qi, h)),
        scratch_shapes=[
            pltpu.VMEM((tq, tk), F32),
            pltpu.VMEM((tq, tk), F32),
            pltpu.VMEM((tq, tk), BF16),
            pltpu.VMEM((tq, tk), BF16),
            pltpu.VMEM((tq, LANES), F32),
            pltpu.VMEM((tq, LANES), F32),
            pltpu.VMEM((2, tq, 1), F32),
            pltpu.VMEM((2, tq, 1), F32),
            pltpu.VMEM((2, tq, hw), F32),
        ],
    )
    return pl.pallas_call(
        kern,
        grid_spec=grid_spec,
        out_shape=jax.ShapeDtypeStruct((b, s, d), BF16),
        compiler_params=_cparams("parallel", "parallel", "parallel", "arbitrary"),
        name="diff_attention_core",
    )(slopes, qn, kn, lam.astype(F32), subln.reshape(1, hw).astype(F32), qkv, qkv, qkv)


POOL_HALO = 16


def _pool_kernel(xp_ref, xc_ref, xn_ref, gin_ref, gout_ref, w_ref, scale_ref, o_ref, *, seq, tm):
    i = pl.program_id(1)
    n = pl.num_programs(1)
    gin = gin_ref[...]
    xc = xc_ref[...]
    uc = _rms(xc, gin)
    up = _rms(xp_ref[...], gin) * (i > 0).astype(F32)
    un = _rms(xn_ref[...], gin) * (i < n - 1).astype(F32)
    grp = w_ref.shape[1]
    hal = POOL_HALO

    r_c = lax.broadcasted_iota(jnp.int32, (tm, tm), 0)
    c_c = lax.broadcasted_iota(jnp.int32, (tm, tm), 1)
    d_c = c_c - r_c
    r_h = lax.broadcasted_iota(jnp.int32, (tm, hal), 0)
    c_h = lax.broadcasted_iota(jnp.int32, (tm, hal), 1)
    d_p = c_h - hal - r_h
    d_n = c_h + tm - r_h
    t_abs = i * tm + lax.broadcasted_iota(jnp.int32, (tm, 1), 0)

    outs = []
    for gi, win in enumerate(POOL_WINDOWS):
        half = win // 2
        sl = slice(gi * grp, (gi + 1) * grp)
        band_c = ((d_c >= -half) & (d_c < half)).astype(F32)
        band_p = (d_p >= -half).astype(F32)
        band_n = (d_n < half).astype(F32)
        wsum = (_dot_exact_lhs(band_c, uc[:, sl]) + _dot_exact_lhs(band_p, up[:, sl])
                + _dot_exact_lhs(band_n, un[:, sl]))
        lo = jnp.clip(t_abs - half, 0, seq)
        hi = jnp.clip(t_abs + half, 0, seq)
        count = (hi - lo).astype(F32)
        pooled = wsum / count - uc[:, sl]
        outs.append(_dot(pooled.astype(BF16), w_ref[gi]))
    y = jnp.concatenate(outs, axis=-1) * scale_ref[...]
    o_ref[...] = xc + _rms(y, gout_ref[...])


def pool_sublayer(x, g_in, g_out, w_groups, scale):
    b, s, d = x.shape
    tm = _pick(s, (256, 128, 64, 32, 16))
    hal = POOL_HALO
    nh = tm // hal
    last = s // hal - 1
    kern = functools.partial(_pool_kernel, seq=s, tm=tm)
    grp = w_groups.shape[1]
    return pl.pallas_call(
        kern,
        grid=(b, s // tm),
        in_specs=[
            pl.BlockSpec((None, hal, d), lambda bi, i: (bi, jnp.maximum(i * nh - 1, 0), 0)),
            pl.BlockSpec((None, tm, d), lambda bi, i: (bi, i, 0)),
            pl.BlockSpec((None, hal, d), lambda bi, i: (bi, jnp.minimum((i + 1) * nh, last), 0)),
            pl.BlockSpec((1, d), lambda bi, i: (0, 0)),
            pl.BlockSpec((1, d), lambda bi, i: (0, 0)),
            pl.BlockSpec((len(POOL_WINDOWS), grp, grp), lambda bi, i: (0, 0, 0)),
            pl.BlockSpec((1, d), lambda bi, i: (0, 0)),
        ],
        out_specs=pl.BlockSpec((None, tm, d), lambda bi, i: (bi, i, 0)),
        out_shape=jax.ShapeDtypeStruct((b, s, d), F32),
        compiler_params=_cparams("parallel", "parallel"),
        name="pool_sublayer",
    )(x, x, x, g_in.reshape(1, d), g_out.reshape(1, d), w_groups, scale.reshape(1, d).astype(F32))


CONV_HALO = 16


def _gdn_conv_kernel(xp_ref, xc_ref, xn_ref, w_ref, o_ref, ext_ref, *, tb, n_q_blocks, n_qk_blocks):
    i = pl.program_id(1)
    j = pl.program_id(2)
    n = pl.num_programs(1)
    hal = CONV_HALO
    left = GDN_CONV // 2
    ext_ref[0:hal, :] = xp_ref[...].astype(F32) * (i > 0).astype(F32)
    ext_ref[hal:hal + tb, :] = xc_ref[...].astype(F32)
    ext_ref[hal + tb:hal + tb + hal, :] = xn_ref[...].astype(F32) * (i < n - 1).astype(F32)
    acc = None
    for kk in range(GDN_CONV):
        start = hal - left + kk
        term = ext_ref[start:start + tb, :] * w_ref[kk:kk + 1, :]
        acc = term if acc is None else acc + term
    y = _silu(acc)

    @pl.when(j >= n_qk_blocks)
    def _():
        o_ref[...] = y.astype(o_ref.dtype)

    @pl.when(j < n_qk_blocks)
    def _():
        qscale = jnp.where(j < n_q_blocks, GDN_K_DIM ** -0.5, 1.0).astype(F32)
        for hh in range(y.shape[1] // GDN_K_DIM):
            sl = slice(hh * GDN_K_DIM, (hh + 1) * GDN_K_DIM)
            t = y[:, sl]
            nrm = lax.rsqrt(jnp.sum(t * t, axis=-1, keepdims=True) + RMS_EPS) * qscale
            o_ref[:, sl] = (t * nrm).astype(o_ref.dtype)


def gdn_conv(qkvz, conv_w, qk_w, conv_dim):
    b, s, _ = qkvz.shape
    tb = _pick(s, (256, 128, 64, 32, 16))
    cb = _pick(qk_w, (1024, 512, 256, 128))
    hal = CONV_HALO
    nh = tb // hal
    last = s // hal - 1
    kern = functools.partial(_gdn_conv_kernel, tb=tb, n_q_blocks=qk_w // cb, n_qk_blocks=2 * qk_w // cb)
    return pl.pallas_call(
        kern,
        grid=(b, s // tb, conv_dim // cb),
        in_specs=[
            pl.BlockSpec((None, hal, cb), lambda bi, i, j: (bi, jnp.maximum(i * nh - 1, 0), j)),
            pl.BlockSpec((None, tb, cb), lambda bi, i, j: (bi, i, j)),
            pl.BlockSpec((None, hal, cb), lambda bi, i, j: (bi, jnp.minimum((i + 1) * nh, last), j)),
            pl.BlockSpec((GDN_CONV, cb), lambda bi, i, j: (0, j)),
        ],
        out_specs=pl.BlockSpec((None, tb, cb), lambda bi, i, j: (bi, i, j)),
        out_shape=jax.ShapeDtypeStruct((b, s, conv_dim), BF16),
        scratch_shapes=[pltpu.VMEM((tb + 2 * hal, cb), F32)],
        compiler_params=_cparams("parallel", "parallel", "parallel"),
        name="gdn_conv",
    )(qkvz, qkvz, qkvz, conv_w.astype(F32))


GDN_HEAD_GROUP = 4


def _split2(x):
    hi = x.astype(BF16)
    lo = (x - hi.astype(F32)).astype(BF16)
    return hi, lo


def _dot_hi_pieces(a_pieces, b_pieces):
    ah, al = a_pieces
    bh, bl = b_pieces
    return _dot(ah, bh) + _dot(ah, bl) + _dot(al, bh)


def _gdn_core_kernel(q_ref, k_ref, v_ref, ab_ref, alog_ref, dt_ref, o_ref, s_ref, *,
                     direction, tb, n_vheads, hg):
    grp = pl.program_id(1)
    blk = pl.program_id(2)
    c = GDN_CHUNK
    nchunk = tb // c
    dk = GDN_K_DIM
    dv = GDN_V_DIM
    assert dk == 2 * c and dv == dk
    hv2 = 2 * n_vheads
    nvh = 2 * hg
    abw = ab_ref.shape[1]

    @pl.when(blk == 0)
    def _():
        s_ref[...] = jnp.zeros_like(s_ref)

    ab = ab_ref[...]
    lane = lax.broadcasted_iota(jnp.int32, ab.shape, 1)
    g_all = -jnp.exp(alog_ref[...]) * jax.nn.softplus(ab + dt_ref[...])
    beta_all = jax.nn.sigmoid(ab)

    r_t = lax.broadcasted_iota(jnp.int32, (tb, tb), 0)
    c_t = lax.broadcasted_iota(jnp.int32, (tb, tb), 1)
    same_chunk = _div_pow2(r_t, c) == _div_pow2(c_t, c)
    tri = (c_t <= r_t) if direction == 0 else (c_t >= r_t)
    cum_mat = (same_chunk & tri).astype(F32)
    gc_all = _dot_exact_lhs(cum_mat, jnp.where(lane < hv2, g_all, 0.0))

    col0 = direction * n_vheads + nvh * grp
    sel_r = lax.broadcasted_iota(jnp.int32, (abw, nvh * dv), 0)
    sel_c = lax.broadcasted_iota(jnp.int32, (abw, nvh * dv), 1)
    a_col = col0 + _div_pow2(sel_c, dv)
    gcb_all = _dot_exact_rhs(gc_all, (sel_r == a_col).astype(F32))
    bb_all = _dot(beta_all.astype(BF16), (sel_r == a_col + hv2).astype(BF16))

    gc_dup = jnp.concatenate([gc_all[(i // 2) * c:(i // 2 + 1) * c] for i in range(2 * nchunk)], axis=0)
    nrow = max(16, nvh)
    rs_r = lax.broadcasted_iota(jnp.int32, (nrow, abw), 0)
    rs_c = lax.broadcasted_iota(jnp.int32, (nrow, abw), 1)
    rsel = (rs_c == col0 + rs_r).astype(BF16)
    d_hi, d_mid, d_lo = _split3(gc_dup)
    rows_g = _dot_nt(rsel, d_hi) + _dot_nt(rsel, d_mid) + _dot_nt(rsel, d_lo)

    ii = lax.broadcasted_iota(jnp.int32, (c, 2 * c), 0)
    ll = lax.broadcasted_iota(jnp.int32, (c, 2 * c), 1)
    jj = jnp.bitwise_and(ll, c - 1)
    second = ll >= c
    if direction == 0:
        m_incl, m_strict, last_row = ii >= jj, ii > jj, c - 1
    else:
        m_incl, m_strict, last_row = ii <= jj, ii < jj, 0
    eye_p = (ii == jj).astype(F32)
    bd_r = lax.broadcasted_iota(jnp.int32, (2 * c, 2 * c), 0)
    bd_c = lax.broadcasted_iota(jnp.int32, (2 * c, 2 * c), 1)
    bd_mask = (bd_r >= c) == (bd_c >= c)
    second_row = lax.broadcasted_iota(jnp.int32, (1, 2 * c), 1) >= c
    lane_sol = lax.broadcasted_iota(jnp.int32, (c, 4 * dv), 1)
    sol_second = jnp.bitwise_and(_div_pow2(lane_sol, dv), 1) == 1
    bd2_r = lax.broadcasted_iota(jnp.int32, (2 * c, 2 * dv), 0)
    bd2_c = lax.broadcasted_iota(jnp.int32, (2 * c, 2 * dv), 1)
    bd2_mask = (bd2_r >= c) == (bd2_c >= dv)
    st_r = lax.broadcasted_iota(jnp.int32, (2 * dk, 2 * dv), 0)
    st_c = lax.broadcasted_iota(jnp.int32, (2 * dk, 2 * dv), 1)
    st_mask = (st_r >= dk) == (st_c >= dv)
    st_second = lax.broadcasted_iota(jnp.int32, (2 * dk, dv), 0) >= dk

    def block_diag(piece):
        return jnp.where(bd_mask, jnp.concatenate([piece, piece], axis=0), 0)

    order = list(range(nchunk)) if direction == 0 else list(range(nchunk - 1, -1, -1))
    probs = [(ch, hh) for ch in order for hh in range(hg)]

    pre = {}
    for ch, hh in probs:
        rows = slice(ch * c, (ch + 1) * c)
        k = k_ref[rows, hh * dk:(hh + 1) * dk]
        q = q_ref[rows, hh * dk:(hh + 1) * dk]
        k2 = jnp.concatenate([k, k], axis=0)
        gram = _dot_nt(jnp.concatenate([k, q], axis=0), k2)
        kk_p = gram[:c]
        qk_p = gram[c:]
        g_full = gcb_all[rows, 2 * hh * dv:(2 * hh + 2) * dv]
        b_full = bb_all[rows, 2 * hh * dv:(2 * hh + 2) * dv]
        g_p = jnp.where(second, g_full[:, dv:dv + 2 * c], g_full[:, :2 * c])
        b_p = jnp.where(second, b_full[:, dv:dv + 2 * c], b_full[:, :2 * c])
        r_blk = slice(ch * 2 * c, (ch + 1) * 2 * c)
        r_row = jnp.where(second_row, rows_g[2 * hh + 1:2 * hh + 2, r_blk], rows_g[2 * hh:2 * hh + 1, r_blk])
        decay = jnp.exp(jnp.where(m_incl, g_p - r_row, NEG_BIG))
        a_p = jnp.where(m_strict, b_p * kk_p * decay, 0.0)
        qkm_p = jnp.where(m_incl, qk_p * decay, 0.0)
        kf2 = jnp.concatenate([k, k], axis=1).astype(F32)
        qf2 = jnp.concatenate([q, q], axis=1).astype(F32)
        eg = jnp.exp(g_full)
        vb = v_ref[rows, hh * 2 * dv:(hh + 1) * 2 * dv].astype(F32) * b_full
        kb = kf2 * (b_full * eg)
        x = jnp.concatenate([vb, kb], axis=1)
        rhs = jnp.concatenate([jnp.where(sol_second, 0.0, x), jnp.where(sol_second, x, 0.0)],
                              axis=0).astype(BF16)
        g_last = g_full[last_row:last_row + 1, :]
        pre[(ch, hh)] = dict(mp=-a_p, t=eye_p - a_p, qkm=qkm_p.astype(BF16), rhs=rhs,
                             qs=(qf2 * eg).astype(BF16), kd=(kf2 * jnp.exp(g_last - g_full)).astype(BF16),
                             egl=jnp.exp(g_last))

    def split_and_diag(p):
        p['mpc'] = _split2(p['mp'])
        p['mbd'] = tuple(block_diag(x) for x in p['mpc'])

    for key in probs:
        split_and_diag(pre[key])
    for key in probs:
        p = pre[key]
        p['mp'] = _dot_hi_pieces(p['mpc'], p['mbd'])
    n_levels = int(math.log2(c)) - 1
    for lvl in range(n_levels):
        last = lvl == n_levels - 1
        for key in probs:
            split_and_diag(pre[key])
        for key in probs:
            p = pre[key]
            mh, ml = p['mpc']
            bh, bl = p['mbd']
            th, tl = _split2(p['t'])
            if last:
                p['t'] = p['t'] + _dot_hi_pieces((th, tl), (bh, bl))
            else:
                by_hi = _dot(jnp.concatenate([mh, ml, th, tl], axis=0), bh)
                by_lo = _dot(jnp.concatenate([mh, th], axis=0), bl)
                p['mp'] = by_hi[:c] + by_hi[c:2 * c] + by_lo[:c]
                p['t'] = p['t'] + (by_hi[2 * c:3 * c] + by_hi[3 * c:] + by_lo[c:])
    for key in probs:
        p = pre[key]
        sol = _dot(p['t'].astype(BF16), p['rhs'])
        p['u'] = sol[:, :2 * dv]
        p['w'] = sol[:, 2 * dv:].astype(BF16)

    for ch in order:
        rows = slice(ch * c, (ch + 1) * c)
        states = [s_ref[hh] for hh in range(hg)]
        sbs = [jnp.where(st_mask, jnp.concatenate([s, s], axis=1), 0.0).astype(BF16) for s in states]
        wq_s = [_dot(jnp.concatenate([pre[(ch, hh)]['w'], pre[(ch, hh)]['qs']], axis=0), sbs[hh])
                for hh in range(hg)]
        for hh in range(hg):
            p = pre[(ch, hh)]
            vnb = (p['u'] - wq_s[hh][:c]).astype(BF16)
            vn2 = jnp.where(bd2_mask, jnp.concatenate([vnb, vnb], axis=0), 0)
            o = wq_s[hh][c:] + _dot(p['qkm'], vn2)
            o_ref[rows, hh * 2 * dv:(hh + 1) * 2 * dv] = o.astype(o_ref.dtype)
            upd = _dot_tn(p['kd'], vnb)
            upd_d = jnp.where(st_second, upd[:, dv:], upd[:, :dv])
            egl = p['egl']
            dl = jnp.where(st_second, jnp.broadcast_to(egl[:, dv:], (2 * dk, dv)),
                           jnp.broadcast_to(egl[:, :dv], (2 * dk, dv)))
            s_ref[hh] = states[hh] * dl + upd_d


def gdn_core(qkv, ab, a_log, dt_bias, direction, n_kheads, n_vheads):
    b, s, _ = qkv.shape
    tb = _pick(s, (256, 128, 64))
    nblk = s // tb
    abw = ab.shape[-1]
    hg = _pick(n_kheads, (GDN_HEAD_GROUP, 2, 1))
    ngrp = n_kheads // hg
    pad = jnp.zeros((2 * n_vheads,), F32)
    alog_row = jnp.concatenate([a_log.reshape(-1).astype(F32), pad]).reshape(1, abw)
    dt_row = jnp.concatenate([dt_bias.reshape(-1).astype(F32), pad]).reshape(1, abw)

    def tok(blk):
        return blk if direction == 0 else nblk - 1 - blk

    kern = functools.partial(_gdn_core_kernel, direction=direction, tb=tb, n_vheads=n_vheads, hg=hg)
    qw = hg * GDN_K_DIM
    vw = hg * 2 * GDN_V_DIM
    return pl.pallas_call(
        kern,
        grid=(b, ngrp, nblk),
        in_specs=[
            pl.BlockSpec((None, tb, qw), lambda bi, g, blk: (bi, tok(blk), g)),
            pl.BlockSpec((None, tb, qw), lambda bi, g, blk: (bi, tok(blk), ngrp + g)),
            pl.BlockSpec((None, tb, vw), lambda bi, g, blk: (bi, tok(blk), ngrp + g)),
            pl.BlockSpec((None, tb, abw), lambda bi, g, blk: (bi, tok(blk), 0)),
            pl.BlockSpec((1, abw), lambda bi, g, blk: (0, 0)),
            pl.BlockSpec((1, abw), lambda bi, g, blk: (0, 0)),
        ],
        out_specs=pl.BlockSpec((None, tb, vw), lambda bi, g, blk: (bi, tok(blk), g)),
        out_shape=jax.ShapeDtypeStruct((b, s, n_vheads * GDN_V_DIM), F32),
        scratch_shapes=[pltpu.VMEM((hg, 2 * GDN_K_DIM, GDN_V_DIM), F32)],
        compiler_params=_cparams("parallel", "parallel", "arbitrary"),
        name="gdn_core_fwd" if direction == 0 else "gdn_core_bwd",
    )(qkv, qkv, qkv, ab, alog_row, dt_row)


def _gdn_gate_kernel(of_ref, ob_ref, z_ref, g_ref, o_ref):
    g = g_ref[...]
    for hh in range(of_ref.shape[1] // GDN_V_DIM):
        sl = slice(hh * GDN_V_DIM, (hh + 1) * GDN_V_DIM)
        o = of_ref[:, sl] + ob_ref[:, sl]
        z = z_ref[:, sl].astype(F32)
        o_ref[:, sl] = (_rms(o, g) * _silu(z)).astype(o_ref.dtype)


def gdn_gate(o_fwd, o_bwd, qkvz, norm_g, z_col0):
    t, w = o_fwd.shape
    tm = _pick(t, (512, 256, 128, 64, 32, 16))
    cb = _pick(w, (1024, 512, 256, 128))
    zoff = z_col0 // cb
    return pl.pallas_call(
        _gdn_gate_kernel,
        grid=(t // tm, w // cb),
        in_specs=[
            pl.BlockSpec((tm, cb), lambda i, j: (i, j)),
            pl.BlockSpec((tm, cb), lambda i, j: (i, j)),
            pl.BlockSpec((tm, cb), lambda i, j: (i, zoff + j)),
            pl.BlockSpec((1, GDN_V_DIM), lambda i, j: (0, 0)),
        ],
        out_specs=pl.BlockSpec((tm, cb), lambda i, j: (i, j)),
        out_shape=jax.ShapeDtypeStruct((t, w), BF16),
        compiler_params=_cparams("parallel", "parallel"),
        name="gdn_gate",
    )(o_fwd, o_bwd, qkvz, norm_g.reshape(1, GDN_V_DIM).astype(F32))


NA_KEY_ROWS = 4
NA_KEY_BLOCKS = 4
NA_BAND_ROWS = NA_KEY_ROWS * NA_KEY_BLOCKS
NA_BAND_LEAD = NA_WIN_ROWS // 2


def _na_kernel(bias_ref, q_ref, *refs, grid_rows):
    k_refs = refs[:NA_KEY_BLOCKS]
    v_refs = refs[NA_KEY_BLOCKS:2 * NA_KEY_BLOCKS]
    o_ref = refs[2 * NA_KEY_BLOCKS]
    i = pl.program_id(2)
    rb = NA_ROW_BLOCK
    tq = rb * GRID_W
    tkb = NA_KEY_ROWS * GRID_W
    q = q_ref[...]
    s = jnp.concatenate([_dot_nt(q, k_ref[...]) for k_ref in k_refs], axis=-1)
    q_row = i * rb + _div_pow2(lax.broadcasted_iota(jnp.int32, (tq, 1), 0), GRID_W)
    win_lo = jnp.clip(q_row - NA_WIN_ROWS // 2, 0, grid_rows - NA_WIN_ROWS)
    k_row = (i * rb - NA_BAND_LEAD
             + _div_pow2(lax.broadcasted_iota(jnp.int32, (1, NA_KEY_BLOCKS * tkb), 1), GRID_W))
    valid = (k_row >= win_lo) & (k_row < win_lo + NA_WIN_ROWS)
    s = jnp.where(valid, s + bias_ref[...], NEG_BIG)
    m = jnp.max(s, axis=-1, keepdims=True)
    p = jnp.exp(s - m)
    l = jnp.sum(p, axis=-1, keepdims=True)
    pb = p.astype(BF16)
    o = _dot(pb[:, :tkb], v_refs[0][...])
    for j in range(1, NA_KEY_BLOCKS):
        o = o + _dot(pb[:, j * tkb:(j + 1) * tkb], v_refs[j][...])
    o_ref[...] = (o / l).astype(o_ref.dtype)


def _na_dense_bias(rpb):
    h = rpb.shape[0]
    rb, w, kw, khw = NA_ROW_BLOCK, GRID_W, NA_WIN_COLS, NA_WIN_ROWS
    cols = jnp.arange(w)
    col_start = jnp.clip(cols - kw // 2, 0, w - kw)
    dc = cols[None, :] - cols[:, None]
    col_ok = (cols[None, :] >= col_start[:, None]) & (cols[None, :] < col_start[:, None] + kw)
    tiles = rpb[:, :, jnp.clip(dc + kw - 1, 0, 2 * kw - 2)]
    tiles = jnp.where(col_ok[None, None], tiles.astype(F32), NEG_BIG)
    tiles = jnp.concatenate([tiles, jnp.full((h, 1, w, w), NEG_BIG, F32)], axis=1)
    rq = jnp.arange(rb)[:, None]
    rk = jnp.arange(NA_BAND_ROWS)[None, :]
    dr = rk - NA_BAND_LEAD - rq
    idx = jnp.where(jnp.abs(dr) <= khw - 1, dr + khw - 1, 2 * khw - 1)
    dense = tiles[:, idx]
    dense = dense.transpose(0, 1, 3, 2, 4).reshape(h, rb * w, NA_BAND_ROWS * w)
    return dense


def na_core(qkv, rpb):
    b, s, d3 = qkv.shape
    d = d3 // 3
    dh = NA_HEAD_DIM
    nh = d // dh
    grid_rows = s // GRID_W
    tq = NA_ROW_BLOCK * GRID_W
    nblk = s // tq
    bias = _na_dense_bias(rpb)
    tkb = NA_KEY_ROWS * GRID_W
    n_kblk = s // tkb
    first = (NA_ROW_BLOCK // NA_KEY_ROWS, NA_BAND_LEAD // NA_KEY_ROWS)

    def band_spec(j, col0):
        return pl.BlockSpec(
            (None, tkb, dh),
            lambda bi, h, i: (bi, jnp.clip(first[0] * i - first[1] + j, 0, n_kblk - 1), col0 + h))

    kern = functools.partial(_na_kernel, grid_rows=grid_rows)
    return pl.pallas_call(
        kern,
        grid=(b, nh, nblk),
        in_specs=[
            pl.BlockSpec((None, tq, NA_BAND_ROWS * GRID_W), lambda bi, h, i: (h, 0, 0)),
            pl.BlockSpec((None, tq, dh), lambda bi, h, i: (bi, i, h)),
            *[band_spec(j, nh) for j in range(NA_KEY_BLOCKS)],
            *[band_spec(j, 2 * nh) for j in range(NA_KEY_BLOCKS)],
        ],
        out_specs=pl.BlockSpec((None, tq, dh), lambda bi, h, i: (bi, i, h)),
        out_shape=jax.ShapeDtypeStruct((b, s, d), BF16),
        compiler_params=_cparams("parallel", "parallel", "parallel"),
        name="na_core",
    )(bias, qkv, *([qkv] * (2 * NA_KEY_BLOCKS)))


def _lambda_init(layer_idx):
    return 0.8 - 0.6 * math.exp(-0.3 * layer_idx)


def _scaled_q_weight(w_qkv, scale):
    d = w_qkv.shape[1] // 3
    return jnp.concatenate([w_qkv[:, :d] * scale, w_qkv[:, d:]], axis=1).astype(BF16)


def _trunk(x, p):
    b, s, d = x.shape
    depth = p['norms'].shape[0]
    xt = x.reshape(b * s, d)
    for i in range(depth):
        g = p['norms'][i]
        xt = ffn_half_step(xt, g[0], g[1], p['ffn_w_in'][i][0], p['ffn_w_out'][i][0])
        kind, j = i % N_MIXERS, i // N_MIXERS
        if kind == 0:
            w_qkv = _scaled_q_weight(p['a_w_qkv'][j], LOG2_E * A_Q_HEAD_DIM ** -0.5)
            qkv = norm_matmul(xt, g[2], w_qkv, BF16).reshape(b, s, 3 * d)
            o = diff_attention_core(qkv, p['a_lambda'][j], p['a_subln'][j], _lambda_init(i))
            xt = matmul_norm_res(o.reshape(b * s, d), p['a_w_out'][j].astype(BF16), g[3], xt)
        elif kind == 1:
            xt = pool_sublayer(xt.reshape(b, s, d), g[2], g[3], p['pool_w'][j].astype(BF16),
                               p['pool_scale'][j]).reshape(b * s, d)
        elif kind == 2:
            n_vheads = p['gdn_a_log'].shape[-1]
            v_w = n_vheads * GDN_V_DIM
            w_in = p['gdn_w_in'][j]
            conv_dim = p['gdn_conv'].shape[-1]
            qk_w = (conv_dim - v_w) // 2
            n_kheads = qk_w // GDN_K_DIM
            qkvz = norm_matmul(xt, g[2], w_in[:, :conv_dim + v_w].astype(BF16), BF16)
            ab = norm_matmul(xt, g[2], w_in[:, conv_dim + v_w:].astype(BF16), F32)
            qkv = gdn_conv(qkvz.reshape(b, s, conv_dim + v_w), p['gdn_conv'][j], qk_w, conv_dim)
            ab3 = ab.reshape(b, s, 4 * n_vheads)
            o_f = gdn_core(qkv, ab3, p['gdn_a_log'][j], p['gdn_dt_bias'][j], 0, n_kheads, n_vheads)
            o_b = gdn_core(qkv, ab3, p['gdn_a_log'][j], p['gdn_dt_bias'][j], 1, n_kheads, n_vheads)
            gated = gdn_gate(o_f.reshape(b * s, v_w), o_b.reshape(b * s, v_w), qkvz,
                             p['gdn_norm'][j], conv_dim)
            xt = matmul_norm_res(gated, p['gdn_w_out'][j].astype(BF16), g[3], xt)
        else:
            w_qkv = _scaled_q_weight(p['na_w_qkv'][j], NA_HEAD_DIM ** -0.5)
            qkv = norm_matmul(xt, g[2], w_qkv, BF16).reshape(b, s, 3 * d)
            o = na_core(qkv, p['na_rpb'][j])
            xt = matmul_norm_res(o.reshape(b * s, d), p['na_w_out'][j].astype(BF16), g[3], xt)
        xt = ffn_half_step(xt, g[4], g[5], p['ffn_w_in'][i][1], p['ffn_w_out'][i][1])
    return xt.reshape(b, s, d)


def kernel(x_prompt, x_sample, norms, ffn_w_in, ffn_w_out, a_w_qkv, a_lambda, a_subln, a_w_out, pool_w, pool_scale, gdn_w_in, gdn_conv, gdn_a_log, gdn_dt_bias, gdn_norm, gdn_w_out, na_w_qkv, na_rpb, na_w_out):
    p = dict(norms=norms, ffn_w_in=ffn_w_in.astype(BF16), ffn_w_out=ffn_w_out.astype(BF16),
             a_w_qkv=a_w_qkv, a_lambda=a_lambda, a_subln=a_subln, a_w_out=a_w_out, pool_w=pool_w,
             pool_scale=pool_scale, gdn_w_in=gdn_w_in, gdn_conv=gdn_conv, gdn_a_log=gdn_a_log,
             gdn_dt_bias=gdn_dt_bias, gdn_norm=gdn_norm, gdn_w_out=gdn_w_out, na_w_qkv=na_w_qkv,
             na_rpb=na_rpb, na_w_out=na_w_out)
    return (_trunk(x_prompt, p), _trunk(x_sample, p))
```

```python
import functools
import math

import jax
import jax.numpy as jnp
from jax import lax
from jax.experimental import pallas as pl
from jax.experimental.pallas import tpu as pltpu

F32 = jnp.float32
BF16 = jnp.bfloat16

RMS_EPS = 1e-6
LOG2_E = math.log2(math.e)
N_MIXERS = 4
GRID_W = 64
A_Q_HEAD_DIM = 128
POOL_WINDOWS = (2, 4, 8, 16)
GDN_K_DIM = 128
GDN_V_DIM = 128
GDN_CONV = 4
GDN_CHUNK = 64
NA_HEAD_DIM = 128
NA_WIN_ROWS = 8
NA_WIN_COLS = 16
NA_ROW_BLOCK = 8
NEG_BIG = -1e30

VMEM_LIMIT_BYTES = 56 * 1024 * 1024


def _cparams(*sem):
    return pltpu.CompilerParams(dimension_semantics=sem, vmem_limit_bytes=VMEM_LIMIT_BYTES)


def _rms(x, g):
    ms = jnp.mean(x * x, axis=-1, keepdims=True)
    return x * lax.rsqrt(ms + RMS_EPS) * g


def _silu(x):
    return x * jax.nn.sigmoid(x)


def _dot(a, b):
    return jnp.dot(a, b, preferred_element_type=F32)


def _dot_nt(a, b):
    return lax.dot_general(a, b, (((1,), (1,)), ((), ())), preferred_element_type=F32)


def _dot_tn(a, b):
    return lax.dot_general(a, b, (((0,), (0,)), ((), ())), preferred_element_type=F32)


def _split3(x):
    hi = x.astype(BF16)
    r1 = x - hi.astype(F32)
    mid = r1.astype(BF16)
    lo = (r1 - mid.astype(F32)).astype(BF16)
    return hi, mid, lo


def _dot_exact_lhs(a_exact, x):
    hi, mid, lo = _split3(x)
    a = a_exact.astype(BF16)
    return _dot(a, hi) + _dot(a, mid) + _dot(a, lo)


def _dot_exact_rhs(x, b_exact):
    hi, mid, lo = _split3(x)
    b = b_exact.astype(BF16)
    return _dot(hi, b) + _dot(mid, b) + _dot(lo, b)


def _dot_hi(a, b):
    ah = a.astype(BF16)
    al = (a - ah.astype(F32)).astype(BF16)
    bh = b.astype(BF16)
    bl = (b - bh.astype(F32)).astype(BF16)
    return _dot(ah, bh) + _dot(ah, bl) + _dot(al, bh)


def _div_pow2(x, n):
    assert n & (n - 1) == 0
    return lax.shift_right_logical(x, int(math.log2(n)))


def _pick(n, prefs):
    for p in prefs:
        if n % p == 0:
            return p
    return n


def _ffn_kernel(x_ref, g0_ref, g1_ref, wg_ref, wu_ref, wo_ref, o_ref, xn_ref):
    j = pl.program_id(1)

    @pl.when(j == 0)
    def _():
        xn_ref[...] = _rms(x_ref[...], g0_ref[...]).astype(BF16)
        o_ref[...] = jnp.zeros_like(o_ref)

    xn = xn_ref[...]
    h = _dot(xn, wg_ref[...])
    u = _dot(xn, wu_ref[...])
    a = (_silu(h) * u).astype(BF16)
    o_ref[...] += _dot(a, wo_ref[...])

    @pl.when(j == pl.num_programs(1) - 1)
    def _():
        o_ref[...] = x_ref[...] + 0.5 * _rms(o_ref[...], g1_ref[...])


def ffn_half_step(x, g0, g1, w_in, w_out):
    t, d = x.shape
    f = w_out.shape[0]
    tm = _pick(t, (512, 256, 128, 64, 32, 16, 8))
    tf = _pick(f, (512, 256, 128))
    nf = f // tf
    return pl.pallas_call(
        _ffn_kernel,
        grid=(t // tm, nf),
        in_specs=[
            pl.BlockSpec((tm, d), lambda i, j: (i, 0)),
            pl.BlockSpec((1, d), lambda i, j: (0, 0)),
            pl.BlockSpec((1, d), lambda i, j: (0, 0)),
            pl.BlockSpec((d, tf), lambda i, j: (0, j)),
            pl.BlockSpec((d, tf), lambda i, j: (0, j + nf)),
            pl.BlockSpec((tf, d), lambda i, j: (j, 0)),
        ],
        out_specs=pl.BlockSpec((tm, d), lambda i, j: (i, 0)),
        out_shape=jax.ShapeDtypeStruct((t, d), F32),
        scratch_shapes=[pltpu.VMEM((tm, d), BF16)],
        compiler_params=_cparams("parallel", "arbitrary"),
        name="ffn_half_step",
    )(x, g0.reshape(1, d), g1.reshape(1, d), w_in, w_in, w_out)


def _norm_matmul_kernel(x_ref, g_ref, w_ref, o_ref, xn_ref):
    @pl.when(pl.program_id(1) == 0)
    def _():
        xn_ref[...] = _rms(x_ref[...], g_ref[...]).astype(BF16)

    o_ref[...] = _dot(xn_ref[...], w_ref[...]).astype(o_ref.dtype)


def norm_matmul(x, g, w, out_dtype):
    t, d = x.shape
    n = w.shape[1]
    tm = _pick(t, (1024, 512, 256, 128, 64, 32, 16, 8))
    tn = _pick(n, (2048, 1024, 512, 256, 128))
    return pl.pallas_call(
        _norm_matmul_kernel,
        grid=(t // tm, n // tn),
        in_specs=[
            pl.BlockSpec((tm, d), lambda i, j: (i, 0)),
            pl.BlockSpec((1, d), lambda i, j: (0, 0)),
            pl.BlockSpec((d, tn), lambda i, j: (0, j)),
        ],
        out_specs=pl.BlockSpec((tm, tn), lambda i, j: (i, j)),
        out_shape=jax.ShapeDtypeStruct((t, n), out_dtype),
        scratch_shapes=[pltpu.VMEM((tm, d), BF16)],
        compiler_params=_cparams("parallel", "arbitrary"),
        name="norm_matmul",
    )(x, g.reshape(1, d), w)


def _matmul_norm_res_kernel(y_ref, w_ref, g_ref, x_ref, o_ref):
    k = pl.program_id(1)

    @pl.when(k == 0)
    def _():
        o_ref[...] = jnp.zeros_like(o_ref)

    o_ref[...] += _dot(y_ref[...], w_ref[...])

    @pl.when(k == pl.num_programs(1) - 1)
    def _():
        o_ref[...] = x_ref[...] + _rms(o_ref[...], g_ref[...])


def matmul_norm_res(y, w, g, x):
    t, kdim = y.shape
    d = w.shape[1]
    tm = _pick(t, (512, 256, 128, 64, 32, 16, 8))
    tk = _pick(kdim, (2048, 1024, 512, 256, 128))
    return pl.pallas_call(
        _matmul_norm_res_kernel,
        grid=(t // tm, kdim // tk),
        in_specs=[
            pl.BlockSpec((tm, tk), lambda i, k: (i, k)),
            pl.BlockSpec((tk, d), lambda i, k: (k, 0)),
            pl.BlockSpec((1, d), lambda i, k: (0, 0)),
            pl.BlockSpec((tm, d), lambda i, k: (i, 0)),
        ],
        out_specs=pl.BlockSpec((tm, d), lambda i, k: (i, 0)),
        out_shape=jax.ShapeDtypeStruct((t, d), F32),
        compiler_params=_cparams("parallel", "arbitrary"),
        name="matmul_norm_res",
    )(y, w, g.reshape(1, d), x)


ATTN_TILE = 1024
ATTN_COL_CHUNK = 256
LANES = 128


ATTN_ROW_BLOCK = 64
ATTN_PV_ROWS = 256


ATTN_ZERO_MARGIN = 160.0


def _attn_all_zero(slopes_ref, qn_ref, kn_ref, bi, h, qi, ki, nq, nk, nh, tq, tk):
    assert tq == tk
    gap = jnp.maximum(jnp.maximum(ki * tk - (qi + 1) * tq + 1, qi * tq - (ki + 1) * tk + 1), 0)
    penalty = slopes_ref[h] * gap.astype(F32)
    zero = gap > 0
    for m in range(2):
        qm = qn_ref[((bi * nq + qi) * nh + h) * 2 + m]
        km = kn_ref[((bi * nk + ki) * nh + h) * 2 + m]
        kd = kn_ref[((bi * nk + qi) * nh + h) * 2 + m]
        zero = jnp.logical_and(zero, penalty > qm * (km + kd) + ATTN_ZERO_MARGIN)
    return zero


def _attn_norms_kernel(x_ref, o_ref):
    ngroups = x_ref.shape[1] // LANES
    lane = lax.broadcasted_iota(jnp.int32, (1, LANES), 1)
    out = jnp.zeros((1, LANES), F32)
    for g in range(ngroups):
        x = x_ref[:, g * LANES:(g + 1) * LANES].astype(F32)
        n2 = jnp.max(jnp.sum(x * x, axis=-1, keepdims=True), axis=0, keepdims=True)
        out = jnp.where(lane == g, jnp.sqrt(n2), out)
    o_ref[...] = out


def attn_block_norms(qkv, width, tile):
    b, s, _ = qkv.shape
    ngroups = width // LANES
    assert ngroups <= LANES
    out = pl.pallas_call(
        _attn_norms_kernel,
        grid=(b, s // tile),
        in_specs=[pl.BlockSpec((None, tile, width), lambda bi, i: (bi, i, 0))],
        out_specs=pl.BlockSpec((None, None, 1, LANES), lambda bi, i: (bi, i, 0, 0)),
        out_shape=jax.ShapeDtypeStruct((b, s // tile, 1, LANES), F32),
        compiler_params=_cparams("parallel", "parallel"),
        name="attn_block_norms",
    )(qkv)
    return out[:, :, 0, :ngroups]


def _diff_attn_kernel(slopes_ref, qn_ref, kn_ref, lam_ref, subln_ref, q_ref, k_ref, v_ref, o_ref,
                      s0_ref, s1_ref, p0_ref, p1_ref, w0_ref, w1_ref, m_ref, l_ref, acc_ref, *,
                      lam_init, tq, tk):
    s_refs, p_refs, w_refs = (s0_ref, s1_ref), (p0_ref, p1_ref), (w0_ref, w1_ref)
    h = pl.program_id(1)
    qi = pl.program_id(2)
    ki = pl.program_id(3)
    dh = A_Q_HEAD_DIM
    cw = ATTN_COL_CHUNK
    rb = ATTN_ROW_BLOCK
    slope = slopes_ref[h]
    nparts = tk // LANES

    @pl.when(ki == 0)
    def _():
        m_ref[...] = jnp.full_like(m_ref, NEG_BIG)
        l_ref[...] = jnp.zeros_like(l_ref)
        acc_ref[...] = jnp.zeros_like(acc_ref)

    delta = qi * tq - ki * tk
    before = delta >= tk
    after = delta <= -tq
    sigma = jnp.where(before, slope, jnp.where(after, -slope, 0.0))

    def query_offsets():
        return lax.broadcasted_iota(jnp.int32, (tq, 1), 0).astype(F32) + delta.astype(F32)

    def scores(overlapping):
        il = query_offsets()
        jl = lax.broadcasted_iota(jnp.int32, (1, tk), 1).astype(F32)
        for m in range(2):
            q = q_ref[:, m * dh:(m + 1) * dh]
            for ci in range(tk // cw):
                cols = slice(ci * cw, (ci + 1) * cw)
                s_c = _dot_nt(q, k_ref[cols, m * dh:(m + 1) * dh])
                if overlapping:
                    s_c = s_c - slope * jnp.abs(il - jl[:, cols])
                else:
                    s_c = s_c + sigma * jl[:, cols]
                s_refs[m][:, cols] = s_c
                part_max = s_c[:, 0:LANES]
                for pt in range(1, cw // LANES):
                    part_max = jnp.maximum(part_max, s_c[:, pt * LANES:(pt + 1) * LANES])
                w_refs[m][...] = part_max if ci == 0 else jnp.maximum(w_refs[m][...], part_max)

    def softmax_and_pv():
        row_off = -sigma * query_offsets()
        for m in range(2):
            s_ref, p_ref, wide_ref = s_refs[m], p_refs[m], w_refs[m]
            m_old = m_ref[m]
            m_new = jnp.maximum(m_old, jnp.max(wide_ref[...], axis=-1, keepdims=True) + row_off)
            alpha = jnp.exp2(m_old - m_new)
            m_ref[m] = m_new
            wide_ref[...] = jnp.broadcast_to(m_new - row_off, (tq, LANES))
            per_group = ATTN_PV_ROWS // rb
            for grp in range(tq // ATTN_PV_ROWS):
                for r in range(grp * per_group, (grp + 1) * per_group):
                    rows = slice(r * rb, (r + 1) * rb)
                    shift_b = wide_ref[rows, :]
                    part_sum = None
                    for pt in range(nparts):
                        lanes = slice(pt * LANES, (pt + 1) * LANES)
                        p = jnp.exp2(s_ref[rows, lanes] - shift_b)
                        p_ref[rows, lanes] = p.astype(BF16)
                        part_sum = p if part_sum is None else part_sum + p
                    wide_ref[rows, :] = part_sum
                grows = slice(grp * ATTN_PV_ROWS, (grp + 1) * ATTN_PV_ROWS)
                acc_ref[m, grows, :] = (alpha[grows] * acc_ref[m, grows, :]
                                        + _dot(p_ref[grows, :], v_ref[...]))
            l_ref[m] = alpha * l_ref[m] + jnp.sum(wide_ref[...], axis=-1, keepdims=True)

    live = jnp.logical_not(_attn_all_zero(slopes_ref, qn_ref, kn_ref, pl.program_id(0), h, qi, ki,
                                          pl.num_programs(2), pl.num_programs(3), pl.num_programs(1),
                                          tq, tk))

    @pl.when(jnp.logical_and(live, jnp.logical_or(before, after)))
    def _():
        scores(False)

    @pl.when(jnp.logical_and(live, jnp.logical_not(jnp.logical_or(before, after))))
    def _():
        scores(True)

    @pl.when(live)
    def _():
        softmax_and_pv()

    @pl.when(ki == pl.num_programs(3) - 1)
    def _():
        lf = lam_ref[...]
        lam_full = (jnp.exp(jnp.sum(lf[0:1] * lf[1:2], axis=-1, keepdims=True))
                    - jnp.exp(jnp.sum(lf[2:3] * lf[3:4], axis=-1, keepdims=True)) + lam_init)
        o = acc_ref[0] / l_ref[0] - lam_full * (acc_ref[1] / l_ref[1])
        o_ref[...] = (_rms(o, subln_ref[...]) * (1.0 - lam_init)).astype(o_ref.dtype)


def diff_attention_core(qkv, lam, subln, lam_init):
    b, s, d3 = qkv.shape
    d = d3 // 3
    hw = 2 * A_Q_HEAD_DIM
    nh = d // hw
    tq = _pick(s, (ATTN_TILE, 512, 256))
    tk = tq
    nq = s // tq
    slopes = LOG2_E * 2.0 ** (-8.0 * jnp.arange(1, nh + 1, dtype=F32) / nh)
    norms = attn_block_norms(qkv, 2 * d, tq)
    qn = norms[:, :, :2 * nh].reshape(-1)
    kn = norms[:, :, 2 * nh:].reshape(-1)
    kern = functools.partial(_diff_attn_kernel, lam_init=lam_init, tq=tq, tk=tk)

    def key_block(bi, h, qi, ki, slopes_ref, qn_ref, kn_ref):
        dead = _attn_all_zero(slopes_ref, qn_ref, kn_ref, bi, h, qi, ki, nq, nq, nh, tq, tk)
        return jnp.where(dead, qi, ki)

    grid_spec = pltpu.PrefetchScalarGridSpec(
        num_scalar_prefetch=3,
        grid=(b, nh, nq, nq),
        in_specs=[
            pl.BlockSpec((4, A_Q_HEAD_DIM), lambda bi, h, qi, ki, *_: (0, 0)),
            pl.BlockSpec((1, hw), lambda bi, h, qi, ki, *_: (0, 0)),
            pl.BlockSpec((None, tq, hw), lambda bi, h, qi, ki, *_: (bi, qi, h)),
            pl.BlockSpec((None, tk, hw), lambda bi, h, qi, ki, *pre: (bi, key_block(bi, h, qi, ki, *pre), nh + h)),
            pl.BlockSpec((None, tk, hw),
                         lambda bi, h, qi, ki, *pre: (bi, key_block(bi, h, qi, ki, *pre), 2 * nh + h)),
        ],
        out_specs=pl.BlockSpec((None, tq, hw), lambda bi, h, qi, ki, *_: (bi, qi, h)),
        scratch_shapes=[
            pltpu.VMEM((tq, tk), F32),
            pltpu.VMEM((tq, tk), F32),
            pltpu.VMEM((tq, tk), BF16),
            pltpu.VMEM((tq, tk), BF16),
            pltpu.VMEM((tq, LANES), F32),
            pltpu.VMEM((tq, LANES), F32),
            pltpu.VMEM((2, tq, 1), F32),
            pltpu.VMEM((2, tq, 1), F32),
            pltpu.VMEM((2, tq, hw), F32),
        ],
    )
    return pl.pallas_call(
        kern,
        grid_spec=grid_spec,
        out_shape=jax.ShapeDtypeStruct((b, s, d), BF16),
        compiler_params=_cparams("parallel", "parallel", "parallel", "arbitrary"),
        name="diff_attention_core",
    )(slopes, qn, kn, lam.astype(F32), subln.reshape(1, hw).astype(F32), qkv, qkv, qkv)


POOL_HALO = 16


def _pool_kernel(xp_ref, xc_ref, xn_ref, gin_ref, gout_ref, w_ref, scale_ref, o_ref, *, seq, tm):
    i = pl.program_id(1)
    n = pl.num_programs(1)
    gin = gin_ref[...]
    xc = xc_ref[...]
    uc = _rms(xc, gin)
    up = _rms(xp_ref[...], gin) * (i > 0).astype(F32)
    un = _rms(xn_ref[...], gin) * (i < n - 1).astype(F32)
    grp = w_ref.shape[1]
    hal = POOL_HALO

    r_c = lax.broadcasted_iota(jnp.int32, (tm, tm), 0)
    c_c = lax.broadcasted_iota(jnp.int32, (tm, tm), 1)
    d_c = c_c - r_c
    r_h = lax.broadcasted_iota(jnp.int32, (tm, hal), 0)
    c_h = lax.broadcasted_iota(jnp.int32, (tm, hal), 1)
    d_p = c_h - hal - r_h
    d_n = c_h + tm - r_h
    t_abs = i * tm + lax.broadcasted_iota(jnp.int32, (tm, 1), 0)

    outs = []
    for gi, win in enumerate(POOL_WINDOWS):
        half = win // 2
        sl = slice(gi * grp, (gi + 1) * grp)
        band_c = ((d_c >= -half) & (d_c < half)).astype(F32)
        band_p = (d_p >= -half).astype(F32)
        band_n = (d_n < half).astype(F32)
        wsum = (_dot_exact_lhs(band_c, uc[:, sl]) + _dot_exact_lhs(band_p, up[:, sl])
                + _dot_exact_lhs(band_n, un[:, sl]))
        lo = jnp.clip(t_abs - half, 0, seq)
        hi = jnp.clip(t_abs + half, 0, seq)
        count = (hi - lo).astype(F32)
        pooled = wsum / count - uc[:, sl]
        outs.append(_dot(pooled.astype(BF16), w_ref[gi]))
    y = jnp.concatenate(outs, axis=-1) * scale_ref[...]
    o_ref[...] = xc + _rms(y, gout_ref[...])


def pool_sublayer(x, g_in, g_out, w_groups, scale):
    b, s, d = x.shape
    tm = _pick(s, (256, 128, 64, 32, 16))
    hal = POOL_HALO
    nh = tm // hal
    last = s // hal - 1
    kern = functools.partial(_pool_kernel, seq=s, tm=tm)
    grp = w_groups.shape[1]
    return pl.pallas_call(
        kern,
        grid=(b, s // tm),
        in_specs=[
            pl.BlockSpec((None, hal, d), lambda bi, i: (bi, jnp.maximum(i * nh - 1, 0), 0)),
            pl.BlockSpec((None, tm, d), lambda bi, i: (bi, i, 0)),
            pl.BlockSpec((None, hal, d), lambda bi, i: (bi, jnp.minimum((i + 1) * nh, last), 0)),
            pl.BlockSpec((1, d), lambda bi, i: (0, 0)),
            pl.BlockSpec((1, d), lambda bi, i: (0, 0)),
            pl.BlockSpec((len(POOL_WINDOWS), grp, grp), lambda bi, i: (0, 0, 0)),
            pl.BlockSpec((1, d), lambda bi, i: (0, 0)),
        ],
        out_specs=pl.BlockSpec((None, tm, d), lambda bi, i: (bi, i, 0)),
        out_shape=jax.ShapeDtypeStruct((b, s, d), F32),
        compiler_params=_cparams("parallel", "parallel"),
        name="pool_sublayer",
    )(x, x, x, g_in.reshape(1, d), g_out.reshape(1, d), w_groups, scale.reshape(1, d).astype(F32))


CONV_HALO = 16


def _gdn_conv_kernel(xp_ref, xc_ref, xn_ref, w_ref, o_ref, ext_ref, *, tb, n_q_blocks, n_qk_blocks):
    i = pl.program_id(1)
    j = pl.program_id(2)
    n = pl.num_programs(1)
    hal = CONV_HALO
    left = GDN_CONV // 2
    ext_ref[0:hal, :] = xp_ref[...].astype(F32) * (i > 0).astype(F32)
    ext_ref[hal:hal + tb, :] = xc_ref[...].astype(F32)
    ext_ref[hal + tb:hal + tb + hal, :] = xn_ref[...].astype(F32) * (i < n - 1).astype(F32)
    acc = None
    for kk in range(GDN_CONV):
        start = hal - left + kk
        term = ext_ref[start:start + tb, :] * w_ref[kk:kk + 1, :]
        acc = term if acc is None else acc + term
    y = _silu(acc)

    @pl.when(j >= n_qk_blocks)
    def _():
        o_ref[...] = y.astype(o_ref.dtype)

    @pl.when(j < n_qk_blocks)
    def _():
        qscale = jnp.where(j < n_q_blocks, GDN_K_DIM ** -0.5, 1.0).astype(F32)
        for hh in range(y.shape[1] // GDN_K_DIM):
            sl = slice(hh * GDN_K_DIM, (hh + 1) * GDN_K_DIM)
            t = y[:, sl]
            nrm = lax.rsqrt(jnp.sum(t * t, axis=-1, keepdims=True) + RMS_EPS) * qscale
            o_ref[:, sl] = (t * nrm).astype(o_ref.dtype)


def gdn_conv(qkvz, conv_w, qk_w, conv_dim):
    b, s, _ = qkvz.shape
    tb = _pick(s, (512, 256, 128, 64, 32, 16))
    cb = _pick(qk_w, (1024, 512, 256, 128))
    hal = CONV_HALO
    nh = tb // hal
    last = s // hal - 1
    kern = functools.partial(_gdn_conv_kernel, tb=tb, n_q_blocks=qk_w // cb, n_qk_blocks=2 * qk_w // cb)
    return pl.pallas_call(
        kern,
        grid=(b, s // tb, conv_dim // cb),
        in_specs=[
            pl.BlockSpec((None, hal, cb), lambda bi, i, j: (bi, jnp.maximum(i * nh - 1, 0), j)),
            pl.BlockSpec((None, tb, cb), lambda bi, i, j: (bi, i, j)),
            pl.BlockSpec((None, hal, cb), lambda bi, i, j: (bi, jnp.minimum((i + 1) * nh, last), j)),
            pl.BlockSpec((GDN_CONV, cb), lambda bi, i, j: (0, j)),
        ],
        out_specs=pl.BlockSpec((None, tb, cb), lambda bi, i, j: (bi, i, j)),
        out_shape=jax.ShapeDtypeStruct((b, s, conv_dim), BF16),
        scratch_shapes=[pltpu.VMEM((tb + 2 * hal, cb), F32)],
        compiler_params=_cparams("parallel", "parallel", "parallel"),
        name="gdn_conv",
    )(qkvz, qkvz, qkvz, conv_w.astype(F32))


GDN_HEAD_GROUP = 4


def _split2(x):
    hi = x.astype(BF16)
    lo = (x - hi.astype(F32)).astype(BF16)
    return hi, lo


def _dot_hi_pieces(a_pieces, b_pieces):
    ah, al = a_pieces
    bh, bl = b_pieces
    return _dot(ah, bh) + _dot(ah, bl) + _dot(al, bh)


def _gdn_core_kernel(q_ref, k_ref, v_ref, ab_ref, alog_ref, dt_ref, o_ref, s_ref, *,
                     direction, tb, n_vheads, hg):
    grp = pl.program_id(1)
    blk = pl.program_id(2)
    c = GDN_CHUNK
    nchunk = tb // c
    dk = GDN_K_DIM
    dv = GDN_V_DIM
    assert dk == 2 * c and dv == dk
    hv2 = 2 * n_vheads
    nvh = 2 * hg
    abw = ab_ref.shape[1]

    @pl.when(blk == 0)
    def _():
        s_ref[...] = jnp.zeros_like(s_ref)

    ab = ab_ref[...]
    lane = lax.broadcasted_iota(jnp.int32, ab.shape, 1)
    g_all = -jnp.exp(alog_ref[...]) * jax.nn.softplus(ab + dt_ref[...])
    beta_all = jax.nn.sigmoid(ab)

    r_t = lax.broadcasted_iota(jnp.int32, (tb, tb), 0)
    c_t = lax.broadcasted_iota(jnp.int32, (tb, tb), 1)
    same_chunk = _div_pow2(r_t, c) == _div_pow2(c_t, c)
    tri = (c_t <= r_t) if direction == 0 else (c_t >= r_t)
    cum_mat = (same_chunk & tri).astype(F32)
    gc_all = _dot_exact_lhs(cum_mat, jnp.where(lane < hv2, g_all, 0.0))

    col0 = direction * n_vheads + nvh * grp
    sel_r = lax.broadcasted_iota(jnp.int32, (abw, nvh * dv), 0)
    sel_c = lax.broadcasted_iota(jnp.int32, (abw, nvh * dv), 1)
    a_col = col0 + _div_pow2(sel_c, dv)
    gcb_all = _dot_exact_rhs(gc_all, (sel_r == a_col).astype(F32))
    bb_all = _dot(beta_all.astype(BF16), (sel_r == a_col + hv2).astype(BF16))

    gc_dup = jnp.concatenate([gc_all[(i // 2) * c:(i // 2 + 1) * c] for i in range(2 * nchunk)], axis=0)
    nrow = max(16, nvh)
    rs_r = lax.broadcasted_iota(jnp.int32, (nrow, abw), 0)
    rs_c = lax.broadcasted_iota(jnp.int32, (nrow, abw), 1)
    rsel = (rs_c == col0 + rs_r).astype(BF16)
    d_hi, d_mid, d_lo = _split3(gc_dup)
    rows_g = _dot_nt(rsel, d_hi) + _dot_nt(rsel, d_mid) + _dot_nt(rsel, d_lo)

    ii = lax.broadcasted_iota(jnp.int32, (c, 2 * c), 0)
    ll = lax.broadcasted_iota(jnp.int32, (c, 2 * c), 1)
    jj = jnp.bitwise_and(ll, c - 1)
    second = ll >= c
    if direction == 0:
        m_incl, m_strict, last_row = ii >= jj, ii > jj, c - 1
    else:
        m_incl, m_strict, last_row = ii <= jj, ii < jj, 0
    eye_p = (ii == jj).astype(F32)
    bd_r = lax.broadcasted_iota(jnp.int32, (2 * c, 2 * c), 0)
    bd_c = lax.broadcasted_iota(jnp.int32, (2 * c, 2 * c), 1)
    bd_mask = (bd_r >= c) == (bd_c >= c)
    second_row = lax.broadcasted_iota(jnp.int32, (1, 2 * c), 1) >= c
    lane_sol = lax.broadcasted_iota(jnp.int32, (c, 4 * dv), 1)
    sol_second = jnp.bitwise_and(_div_pow2(lane_sol, dv), 1) == 1
    bd2_r = lax.broadcasted_iota(jnp.int32, (2 * c, 2 * dv), 0)
    bd2_c = lax.broadcasted_iota(jnp.int32, (2 * c, 2 * dv), 1)
    bd2_mask = (bd2_r >= c) == (bd2_c >= dv)
    st_r = lax.broadcasted_iota(jnp.int32, (2 * dk, 2 * dv), 0)
    st_c = lax.broadcasted_iota(jnp.int32, (2 * dk, 2 * dv), 1)
    st_mask = (st_r >= dk) == (st_c >= dv)
    st_second = lax.broadcasted_iota(jnp.int32, (2 * dk, dv), 0) >= dk

    def block_diag(piece):
        return jnp.where(bd_mask, jnp.concatenate([piece, piece], axis=0), 0)

    order = list(range(nchunk)) if direction == 0 else list(range(nchunk - 1, -1, -1))
    probs = [(ch, hh) for ch in order for hh in range(hg)]

    pre = {}
    for ch, hh in probs:
        rows = slice(ch * c, (ch + 1) * c)
        k = k_ref[rows, hh * dk:(hh + 1) * dk]
        q = q_ref[rows, hh * dk:(hh + 1) * dk]
        k2 = jnp.concatenate([k, k], axis=0)
        gram = _dot_nt(jnp.concatenate([k, q], axis=0), k2)
        kk_p = gram[:c]
        qk_p = gram[c:]
        g_full = gcb_all[rows, 2 * hh * dv:(2 * hh + 2) * dv]
        b_full = bb_all[rows, 2 * hh * dv:(2 * hh + 2) * dv]
        g_p = jnp.where(second, g_full[:, dv:dv + 2 * c], g_full[:, :2 * c])
        b_p = jnp.where(second, b_full[:, dv:dv + 2 * c], b_full[:, :2 * c])
        r_blk = slice(ch * 2 * c, (ch + 1) * 2 * c)
        r_row = jnp.where(second_row, rows_g[2 * hh + 1:2 * hh + 2, r_blk], rows_g[2 * hh:2 * hh + 1, r_blk])
        decay = jnp.exp(jnp.where(m_incl, g_p - r_row, NEG_BIG))
        a_p = jnp.where(m_strict, b_p * kk_p * decay, 0.0)
        qkm_p = jnp.where(m_incl, qk_p * decay, 0.0)
        kf2 = jnp.concatenate([k, k], axis=1).astype(F32)
        qf2 = jnp.concatenate([q, q], axis=1).astype(F32)
        eg = jnp.exp(g_full)
        vb = v_ref[rows, hh * 2 * dv:(hh + 1) * 2 * dv].astype(F32) * b_full
        kb = kf2 * (b_full * eg)
        x = jnp.concatenate([vb, kb], axis=1)
        rhs = jnp.concatenate([jnp.where(sol_second, 0.0, x), jnp.where(sol_second, x, 0.0)],
                              axis=0).astype(BF16)
        g_last = g_full[last_row:last_row + 1, :]
        pre[(ch, hh)] = dict(mp=-a_p, t=eye_p - a_p, qkm=qkm_p.astype(BF16), rhs=rhs,
                             qs=(qf2 * eg).astype(BF16), kd=(kf2 * jnp.exp(g_last - g_full)).astype(BF16),
                             egl=jnp.exp(g_last))

    def split_and_diag(p):
        p['mpc'] = _split2(p['mp'])
        p['mbd'] = tuple(block_diag(x) for x in p['mpc'])

    for key in probs:
        split_and_diag(pre[key])
    for key in probs:
        p = pre[key]
        p['mp'] = _dot_hi_pieces(p['mpc'], p['mbd'])
    n_levels = int(math.log2(c)) - 1
    for lvl in range(n_levels):
        last = lvl == n_levels - 1
        for key in probs:
            split_and_diag(pre[key])
        for key in probs:
            p = pre[key]
            mh, ml = p['mpc']
            bh, bl = p['mbd']
            th, tl = _split2(p['t'])
            if last:
                p['t'] = p['t'] + _dot(th, bh)
            else:
                by_hi = _dot(jnp.concatenate([mh, ml, th, tl], axis=0), bh)
                by_lo = _dot(jnp.concatenate([mh, th], axis=0), bl)
                p['mp'] = by_hi[:c] + by_hi[c:2 * c] + by_lo[:c]
                p['t'] = p['t'] + (by_hi[2 * c:3 * c] + by_hi[3 * c:] + by_lo[c:])
    for key in probs:
        p = pre[key]
        sol = _dot(p['t'].astype(BF16), p['rhs'])
        p['u'] = sol[:, :2 * dv]
        p['w'] = sol[:, 2 * dv:].astype(BF16)

    for ch in order:
        rows = slice(ch * c, (ch + 1) * c)
        states = [s_ref[hh] for hh in range(hg)]
        sbs = [jnp.where(st_mask, jnp.concatenate([s, s], axis=1), 0.0).astype(BF16) for s in states]
        wq_s = [_dot(jnp.concatenate([pre[(ch, hh)]['w'], pre[(ch, hh)]['qs']], axis=0), sbs[hh])
                for hh in range(hg)]
        for hh in range(hg):
            p = pre[(ch, hh)]
            vnb = (p['u'] - wq_s[hh][:c]).astype(BF16)
            vn2 = jnp.where(bd2_mask, jnp.concatenate([vnb, vnb], axis=0), 0)
            o = wq_s[hh][c:] + _dot(p['qkm'], vn2)
            o_ref[rows, hh * 2 * dv:(hh + 1) * 2 * dv] = o.astype(o_ref.dtype)
            upd = _dot_tn(p['kd'], vnb)
            upd_d = jnp.where(st_second, upd[:, dv:], upd[:, :dv])
            egl = p['egl']
            dl = jnp.where(st_second, jnp.broadcast_to(egl[:, dv:], (2 * dk, dv)),
                           jnp.broadcast_to(egl[:, :dv], (2 * dk, dv)))
            s_ref[hh] = states[hh] * dl + upd_d


def gdn_core(qkv, ab, a_log, dt_bias, direction, n_kheads, n_vheads):
    b, s, _ = qkv.shape
    tb = _pick(s, (256, 128, 64))
    nblk = s // tb
    abw = ab.shape[-1]
    hg = _pick(n_kheads, (GDN_HEAD_GROUP, 2, 1))
    ngrp = n_kheads // hg
    pad = jnp.zeros((2 * n_vheads,), F32)
    alog_row = jnp.concatenate([a_log.reshape(-1).astype(F32), pad]).reshape(1, abw)
    dt_row = jnp.concatenate([dt_bias.reshape(-1).astype(F32), pad]).reshape(1, abw)

    def tok(blk):
        return blk if direction == 0 else nblk - 1 - blk

    kern = functools.partial(_gdn_core_kernel, direction=direction, tb=tb, n_vheads=n_vheads, hg=hg)
    qw = hg * GDN_K_DIM
    vw = hg * 2 * GDN_V_DIM
    return pl.pallas_call(
        kern,
        grid=(b, ngrp, nblk),
        in_specs=[
            pl.BlockSpec((None, tb, qw), lambda bi, g, blk: (bi, tok(blk), g)),
            pl.BlockSpec((None, tb, qw), lambda bi, g, blk: (bi, tok(blk), ngrp + g)),
            pl.BlockSpec((None, tb, vw), lambda bi, g, blk: (bi, tok(blk), ngrp + g)),
            pl.BlockSpec((None, tb, abw), lambda bi, g, blk: (bi, tok(blk), 0)),
            pl.BlockSpec((1, abw), lambda bi, g, blk: (0, 0)),
            pl.BlockSpec((1, abw), lambda bi, g, blk: (0, 0)),
        ],
        out_specs=pl.BlockSpec((None, tb, vw), lambda bi, g, blk: (bi, tok(blk), g)),
        out_shape=jax.ShapeDtypeStruct((b, s, n_vheads * GDN_V_DIM), F32),
        scratch_shapes=[pltpu.VMEM((hg, 2 * GDN_K_DIM, GDN_V_DIM), F32)],
        compiler_params=_cparams("parallel", "parallel", "arbitrary"),
        name="gdn_core_fwd" if direction == 0 else "gdn_core_bwd",
    )(qkv, qkv, qkv, ab, alog_row, dt_row)


def _gdn_gate_kernel(of_ref, ob_ref, z_ref, g_ref, o_ref):
    g = g_ref[...]
    for hh in range(of_ref.shape[1] // GDN_V_DIM):
        sl = slice(hh * GDN_V_DIM, (hh + 1) * GDN_V_DIM)
        o = of_ref[:, sl] + ob_ref[:, sl]
        z = z_ref[:, sl].astype(F32)
        o_ref[:, sl] = (_rms(o, g) * _silu(z)).astype(o_ref.dtype)


def gdn_gate(o_fwd, o_bwd, qkvz, norm_g, z_col0):
    t, w = o_fwd.shape
    tm = _pick(t, (512, 256, 128, 64, 32, 16))
    cb = _pick(w, (1024, 512, 256, 128))
    zoff = z_col0 // cb
    return pl.pallas_call(
        _gdn_gate_kernel,
        grid=(t // tm, w // cb),
        in_specs=[
            pl.BlockSpec((tm, cb), lambda i, j: (i, j)),
            pl.BlockSpec((tm, cb), lambda i, j: (i, j)),
            pl.BlockSpec((tm, cb), lambda i, j: (i, zoff + j)),
            pl.BlockSpec((1, GDN_V_DIM), lambda i, j: (0, 0)),
        ],
        out_specs=pl.BlockSpec((tm, cb), lambda i, j: (i, j)),
        out_shape=jax.ShapeDtypeStruct((t, w), BF16),
        compiler_params=_cparams("parallel", "parallel"),
        name="gdn_gate",
    )(o_fwd, o_bwd, qkvz, norm_g.reshape(1, GDN_V_DIM).astype(F32))


NA_KEY_ROWS = 4
NA_KEY_BLOCKS = 4
NA_BAND_ROWS = NA_KEY_ROWS * NA_KEY_BLOCKS
NA_BAND_LEAD = NA_WIN_ROWS // 2


def _na_kernel(bias_ref, q_ref, *refs, grid_rows):
    k_refs = refs[:NA_KEY_BLOCKS]
    v_refs = refs[NA_KEY_BLOCKS:2 * NA_KEY_BLOCKS]
    o_ref = refs[2 * NA_KEY_BLOCKS]
    i = pl.program_id(2)
    rb = NA_ROW_BLOCK
    tq = rb * GRID_W
    tkb = NA_KEY_ROWS * GRID_W
    q = q_ref[...]
    s = jnp.concatenate([_dot_nt(q, k_ref[...]) for k_ref in k_refs], axis=-1)
    q_row = i * rb + _div_pow2(lax.broadcasted_iota(jnp.int32, (tq, 1), 0), GRID_W)
    win_lo = jnp.clip(q_row - NA_WIN_ROWS // 2, 0, grid_rows - NA_WIN_ROWS)
    k_row = (i * rb - NA_BAND_LEAD
             + _div_pow2(lax.broadcasted_iota(jnp.int32, (1, NA_KEY_BLOCKS * tkb), 1), GRID_W))
    valid = (k_row >= win_lo) & (k_row < win_lo + NA_WIN_ROWS)
    s = jnp.where(valid, s + bias_ref[...], NEG_BIG)
    m = jnp.max(s, axis=-1, keepdims=True)
    p = jnp.exp(s - m)
    l = jnp.sum(p, axis=-1, keepdims=True)
    pb = p.astype(BF16)
    o = _dot(pb[:, :tkb], v_refs[0][...])
    for j in range(1, NA_KEY_BLOCKS):
        o = o + _dot(pb[:, j * tkb:(j + 1) * tkb], v_refs[j][...])
    o_ref[...] = (o / l).astype(o_ref.dtype)


def _na_dense_bias(rpb):
    h = rpb.shape[0]
    rb, w, kw, khw = NA_ROW_BLOCK, GRID_W, NA_WIN_COLS, NA_WIN_ROWS
    cols = jnp.arange(w)
    col_start = jnp.clip(cols - kw // 2, 0, w - kw)
    dc = cols[None, :] - cols[:, None]
    col_ok = (cols[None, :] >= col_start[:, None]) & (cols[None, :] < col_start[:, None] + kw)
    tiles = rpb[:, :, jnp.clip(dc + kw - 1, 0, 2 * kw - 2)]
    tiles = jnp.where(col_ok[None, None], tiles.astype(F32), NEG_BIG)
    tiles = jnp.concatenate([tiles, jnp.full((h, 1, w, w), NEG_BIG, F32)], axis=1)
    rq = jnp.arange(rb)[:, None]
    rk = jnp.arange(NA_BAND_ROWS)[None, :]
    dr = rk - NA_BAND_LEAD - rq
    idx = jnp.where(jnp.abs(dr) <= khw - 1, dr + khw - 1, 2 * khw - 1)
    dense = tiles[:, idx]
    dense = dense.transpose(0, 1, 3, 2, 4).reshape(h, rb * w, NA_BAND_ROWS * w)
    return dense


def na_core(qkv, rpb):
    b, s, d3 = qkv.shape
    d = d3 // 3
    dh = NA_HEAD_DIM
    nh = d // dh
    grid_rows = s // GRID_W
    tq = NA_ROW_BLOCK * GRID_W
    nblk = s // tq
    bias = _na_dense_bias(rpb)
    tkb = NA_KEY_ROWS * GRID_W
    n_kblk = s // tkb
    first = (NA_ROW_BLOCK // NA_KEY_ROWS, NA_BAND_LEAD // NA_KEY_ROWS)

    def band_spec(j, col0):
        return pl.BlockSpec(
            (None, tkb, dh),
            lambda bi, h, i: (bi, jnp.clip(first[0] * i - first[1] + j, 0, n_kblk - 1), col0 + h))

    kern = functools.partial(_na_kernel, grid_rows=grid_rows)
    return pl.pallas_call(
        kern,
        grid=(b, nh, nblk),
        in_specs=[
            pl.BlockSpec((None, tq, NA_BAND_ROWS * GRID_W), lambda bi, h, i: (h, 0, 0)),
            pl.BlockSpec((None, tq, dh), lambda bi, h, i: (bi, i, h)),
            *[band_spec(j, nh) for j in range(NA_KEY_BLOCKS)],
            *[band_spec(j, 2 * nh) for j in range(NA_KEY_BLOCKS)],
        ],
        out_specs=pl.BlockSpec((None, tq, dh), lambda bi, h, i: (bi, i, h)),
        out_shape=jax.ShapeDtypeStruct((b, s, d), BF16),
        compiler_params=_cparams("parallel", "parallel", "parallel"),
        name="na_core",
    )(bias, qkv, *([qkv] * (2 * NA_KEY_BLOCKS)))


def _lambda_init(layer_idx):
    return 0.8 - 0.6 * math.exp(-0.3 * layer_idx)


def _scaled_q_weight(w_qkv, scale):
    d = w_qkv.shape[1] // 3
    return jnp.concatenate([w_qkv[:, :d] * scale, w_qkv[:, d:]], axis=1).astype(BF16)


def _trunk(x, p):
    b, s, d = x.shape
    depth = p['norms'].shape[0]
    xt = x.reshape(b * s, d)
    for i in range(depth):
        g = p['norms'][i]
        xt = ffn_half_step(xt, g[0], g[1], p['ffn_w_in'][i][0], p['ffn_w_out'][i][0])
        kind, j = i % N_MIXERS, i // N_MIXERS
        if kind == 0:
            w_qkv = _scaled_q_weight(p['a_w_qkv'][j], LOG2_E * A_Q_HEAD_DIM ** -0.5)
            qkv = norm_matmul(xt, g[2], w_qkv, BF16).reshape(b, s, 3 * d)
            o = diff_attention_core(qkv, p['a_lambda'][j], p['a_subln'][j], _lambda_init(i))
            xt = matmul_norm_res(o.reshape(b * s, d), p['a_w_out'][j].astype(BF16), g[3], xt)
        elif kind == 1:
            xt = pool_sublayer(xt.reshape(b, s, d), g[2], g[3], p['pool_w'][j].astype(BF16),
                               p['pool_scale'][j]).reshape(b * s, d)
        elif kind == 2:
            n_vheads = p['gdn_a_log'].shape[-1]
            v_w = n_vheads * GDN_V_DIM
            w_in = p['gdn_w_in'][j]
            conv_dim = p['gdn_conv'].shape[-1]
            qk_w = (conv_dim - v_w) // 2
            n_kheads = qk_w // GDN_K_DIM
            qkvz = norm_matmul(xt, g[2], w_in[:, :conv_dim + v_w].astype(BF16), BF16)
            ab = norm_matmul(xt, g[2], w_in[:, conv_dim + v_w:].astype(BF16), F32)
            qkv = gdn_conv(qkvz.reshape(b, s, conv_dim + v_w), p['gdn_conv'][j], qk_w, conv_dim)
            ab3 = ab.reshape(b, s, 4 * n_vheads)
            o_f = gdn_core(qkv, ab3, p['gdn_a_log'][j], p['gdn_dt_bias'][j], 0, n_kheads, n_vheads)
            o_b = gdn_core(qkv, ab3, p['gdn_a_log'][j], p['gdn_dt_bias'][j], 1, n_kheads, n_vheads)
            gated = gdn_gate(o_f.reshape(b * s, v_w), o_b.reshape(b * s, v_w), qkvz,
                             p['gdn_norm'][j], conv_dim)
            xt = matmul_norm_res(gated, p['gdn_w_out'][j].astype(BF16), g[3], xt)
        else:
            w_qkv = _scaled_q_weight(p['na_w_qkv'][j], NA_HEAD_DIM ** -0.5)
            qkv = norm_matmul(xt, g[2], w_qkv, BF16).reshape(b, s, 3 * d)
            o = na_core(qkv, p['na_rpb'][j])
            xt = matmul_norm_res(o.reshape(b * s, d), p['na_w_out'][j].astype(BF16), g[3], xt)
        xt = ffn_half_step(xt, g[4], g[5], p['ffn_w_in'][i][1], p['ffn_w_out'][i][1])
    return xt.reshape(b, s, d)


def kernel(x_prompt, x_sample, norms, ffn_w_in, ffn_w_out, a_w_qkv, a_lambda, a_subln, a_w_out, pool_w, pool_scale, gdn_w_in, gdn_conv, gdn_a_log, gdn_dt_bias, gdn_norm, gdn_w_out, na_w_qkv, na_rpb, na_w_out):
    p = dict(norms=norms, ffn_w_in=ffn_w_in.astype(BF16), ffn_w_out=ffn_w_out.astype(BF16),
             a_w_qkv=a_w_qkv, a_lambda=a_lambda, a_subln=a_subln, a_w_out=a_w_out, pool_w=pool_w,
             pool_scale=pool_scale, gdn_w_in=gdn_w_in, gdn_conv=gdn_conv, gdn_a_log=gdn_a_log,
             gdn_dt_bias=gdn_dt_bias, gdn_norm=gdn_norm, gdn_w_out=gdn_w_out, na_w_qkv=na_w_qkv,
             na_rpb=na_rpb, na_w_out=na_w_out)
    return (_trunk(x_prompt, p), _trunk(x_sample, p))
```

```python
import functools
import math

import jax
import jax.numpy as jnp
from jax import lax
from jax.experimental import pallas as pl
from jax.experimental.pallas import tpu as pltpu

F32 = jnp.float32
BF16 = jnp.bfloat16

RMS_EPS = 1e-6
LOG2_E = math.log2(math.e)
N_MIXERS = 4
GRID_W = 64
A_Q_HEAD_DIM = 128
POOL_WINDOWS = (2, 4, 8, 16)
GDN_K_DIM = 128
GDN_V_DIM = 128
GDN_CONV = 4
GDN_CHUNK = 64
NA_HEAD_DIM = 128
NA_WIN_ROWS = 8
NA_WIN_COLS = 16
NA_ROW_BLOCK = 8
NEG_BIG = -1e30

VMEM_LIMIT_BYTES = 56 * 1024 * 1024


def _cparams(*sem):
    return pltpu.CompilerParams(dimension_semantics=sem, vmem_limit_bytes=VMEM_LIMIT_BYTES)


def _rms(x, g):
    ms = jnp.mean(x * x, axis=-1, keepdims=True)
    return x * lax.rsqrt(ms + RMS_EPS) * g


def _silu(x):
    return x * jax.nn.sigmoid(x)


def _dot(a, b):
    return jnp.dot(a, b, preferred_element_type=F32)


def _dot_nt(a, b):
    return lax.dot_general(a, b, (((1,), (1,)), ((), ())), preferred_element_type=F32)


def _dot_tn(a, b):
    return lax.dot_general(a, b, (((0,), (0,)), ((), ())), preferred_element_type=F32)


def _split3(x):
    hi = x.astype(BF16)
    r1 = x - hi.astype(F32)
    mid = r1.astype(BF16)
    lo = (r1 - mid.astype(F32)).astype(BF16)
    return hi, mid, lo


def _dot_exact_lhs(a_exact, x):
    hi, mid, lo = _split3(x)
    a = a_exact.astype(BF16)
    return _dot(a, hi) + _dot(a, mid) + _dot(a, lo)


def _dot_exact_rhs(x, b_exact):
    hi, mid, lo = _split3(x)
    b = b_exact.astype(BF16)
    return _dot(hi, b) + _dot(mid, b) + _dot(lo, b)


def _dot_hi(a, b):
    ah = a.astype(BF16)
    al = (a - ah.astype(F32)).astype(BF16)
    bh = b.astype(BF16)
    bl = (b - bh.astype(F32)).astype(BF16)
    return _dot(ah, bh) + _dot(ah, bl) + _dot(al, bh)


def _div_pow2(x, n):
    assert n & (n - 1) == 0
    return lax.shift_right_logical(x, int(math.log2(n)))


def _pick(n, prefs):
    for p in prefs:
        if n % p == 0:
            return p
    return n


def _ffn_kernel(x_ref, g0_ref, g1_ref, wg_ref, wu_ref, wo_ref, o_ref, xn_ref):
    j = pl.program_id(1)

    @pl.when(j == 0)
    def _():
        xn_ref[...] = _rms(x_ref[...], g0_ref[...]).astype(BF16)
        o_ref[...] = jnp.zeros_like(o_ref)

    xn = xn_ref[...]
    h = _dot(xn, wg_ref[...])
    u = _dot(xn, wu_ref[...])
    a = (_silu(h) * u).astype(BF16)
    o_ref[...] += _dot(a, wo_ref[...])

    @pl.when(j == pl.num_programs(1) - 1)
    def _():
        o_ref[...] = x_ref[...] + 0.5 * _rms(o_ref[...], g1_ref[...])


def ffn_half_step(x, g0, g1, w_in, w_out):
    t, d = x.shape
    f = w_out.shape[0]
    tm = _pick(t, (512, 256, 128, 64, 32, 16, 8))
    tf = _pick(f, (512, 256, 128))
    nf = f // tf
    return pl.pallas_call(
        _ffn_kernel,
        grid=(t // tm, nf),
        in_specs=[
            pl.BlockSpec((tm, d), lambda i, j: (i, 0)),
            pl.BlockSpec((1, d), lambda i, j: (0, 0)),
            pl.BlockSpec((1, d), lambda i, j: (0, 0)),
            pl.BlockSpec((d, tf), lambda i, j: (0, j)),
            pl.BlockSpec((d, tf), lambda i, j: (0, j + nf)),
            pl.BlockSpec((tf, d), lambda i, j: (j, 0)),
        ],
        out_specs=pl.BlockSpec((tm, d), lambda i, j: (i, 0)),
        out_shape=jax.ShapeDtypeStruct((t, d), F32),
        scratch_shapes=[pltpu.VMEM((tm, d), BF16)],
        compiler_params=_cparams("parallel", "arbitrary"),
        name="ffn_half_step",
    )(x, g0.reshape(1, d), g1.reshape(1, d), w_in, w_in, w_out)


def _norm_matmul_kernel(x_ref, g_ref, w_ref, o_ref, xn_ref):
    @pl.when(pl.program_id(1) == 0)
    def _():
        xn_ref[...] = _rms(x_ref[...], g_ref[...]).astype(BF16)

    o_ref[...] = _dot(xn_ref[...], w_ref[...]).astype(o_ref.dtype)


def norm_matmul(x, g, w, out_dtype):
    t, d = x.shape
    n = w.shape[1]
    tm = _pick(t, (1024, 512, 256, 128, 64, 32, 16, 8))
    tn = _pick(n, (2048, 1024, 512, 256, 128))
    return pl.pallas_call(
        _norm_matmul_kernel,
        grid=(t // tm, n // tn),
        in_specs=[
            pl.BlockSpec((tm, d), lambda i, j: (i, 0)),
            pl.BlockSpec((1, d), lambda i, j: (0, 0)),
            pl.BlockSpec((d, tn), lambda i, j: (0, j)),
        ],
        out_specs=pl.BlockSpec((tm, tn), lambda i, j: (i, j)),
        out_shape=jax.ShapeDtypeStruct((t, n), out_dtype),
        scratch_shapes=[pltpu.VMEM((tm, d), BF16)],
        compiler_params=_cparams("parallel", "arbitrary"),
        name="norm_matmul",
    )(x, g.reshape(1, d), w)


def _matmul_norm_res_kernel(y_ref, w_ref, g_ref, x_ref, o_ref):
    k = pl.program_id(1)

    @pl.when(k == 0)
    def _():
        o_ref[...] = jnp.zeros_like(o_ref)

    o_ref[...] += _dot(y_ref[...], w_ref[...])

    @pl.when(k == pl.num_programs(1) - 1)
    def _():
        o_ref[...] = x_ref[...] + _rms(o_ref[...], g_ref[...])


def matmul_norm_res(y, w, g, x):
    t, kdim = y.shape
    d = w.shape[1]
    tm = _pick(t, (512, 256, 128, 64, 32, 16, 8))
    tk = _pick(kdim, (2048, 1024, 512, 256, 128))
    return pl.pallas_call(
        _matmul_norm_res_kernel,
        grid=(t // tm, kdim // tk),
        in_specs=[
            pl.BlockSpec((tm, tk), lambda i, k: (i, k)),
            pl.BlockSpec((tk, d), lambda i, k: (k, 0)),
            pl.BlockSpec((1, d), lambda i, k: (0, 0)),
            pl.BlockSpec((tm, d), lambda i, k: (i, 0)),
        ],
        out_specs=pl.BlockSpec((tm, d), lambda i, k: (i, 0)),
        out_shape=jax.ShapeDtypeStruct((t, d), F32),
        compiler_params=_cparams("parallel", "arbitrary"),
        name="matmul_norm_res",
    )(y, w, g.reshape(1, d), x)


ATTN_TILE = 1024
ATTN_COL_CHUNK = 256
LANES = 128


ATTN_ROW_BLOCK = 64
ATTN_PV_ROWS = 256


ATTN_ZERO_MARGIN = 160.0


def _attn_all_zero(slopes_ref, qn_ref, kn_ref, bi, h, qi, ki, nq, nk, nh, tq, tk):
    assert tq == tk
    gap = jnp.maximum(jnp.maximum(ki * tk - (qi + 1) * tq + 1, qi * tq - (ki + 1) * tk + 1), 0)
    penalty = slopes_ref[h] * gap.astype(F32)
    zero = gap > 0
    for m in range(2):
        qm = qn_ref[((bi * nq + qi) * nh + h) * 2 + m]
        km = kn_ref[((bi * nk + ki) * nh + h) * 2 + m]
        kd = kn_ref[((bi * nk + qi) * nh + h) * 2 + m]
        zero = jnp.logical_and(zero, penalty > qm * (km + kd) + ATTN_ZERO_MARGIN)
    return zero


def _attn_norms_kernel(x_ref, o_ref):
    ngroups = x_ref.shape[1] // LANES
    lane = lax.broadcasted_iota(jnp.int32, (1, LANES), 1)
    out = jnp.zeros((1, LANES), F32)
    for g in range(ngroups):
        x = x_ref[:, g * LANES:(g + 1) * LANES].astype(F32)
        n2 = jnp.max(jnp.sum(x * x, axis=-1, keepdims=True), axis=0, keepdims=True)
        out = jnp.where(lane == g, jnp.sqrt(n2), out)
    o_ref[...] = out


def attn_block_norms(qkv, width, tile):
    b, s, _ = qkv.shape
    ngroups = width // LANES
    assert ngroups <= LANES
    out = pl.pallas_call(
        _attn_norms_kernel,
        grid=(b, s // tile),
        in_specs=[pl.BlockSpec((None, tile, width), lambda bi, i: (bi, i, 0))],
        out_specs=pl.BlockSpec((None, None, 1, LANES), lambda bi, i: (bi, i, 0, 0)),
        out_shape=jax.ShapeDtypeStruct((b, s // tile, 1, LANES), F32),
        compiler_params=_cparams("parallel", "parallel"),
        name="attn_block_norms",
    )(qkv)
    return out[:, :, 0, :ngroups]


def _diff_attn_kernel(slopes_ref, qn_ref, kn_ref, lam_ref, subln_ref, q_ref, k_ref, v_ref, o_ref,
                      s0_ref, s1_ref, p0_ref, p1_ref, w0_ref, w1_ref, m_ref, l_ref, acc_ref, *,
                      lam_init, tq, tk):
    s_refs, p_refs, w_refs = (s0_ref, s1_ref), (p0_ref, p1_ref), (w0_ref, w1_ref)
    h = pl.program_id(1)
    qi = pl.program_id(2)
    ki = pl.program_id(3)
    dh = A_Q_HEAD_DIM
    cw = ATTN_COL_CHUNK
    rb = ATTN_ROW_BLOCK
    slope = slopes_ref[h]
    nparts = tk // LANES

    @pl.when(ki == 0)
    def _():
        m_ref[...] = jnp.full_like(m_ref, NEG_BIG)
        l_ref[...] = jnp.zeros_like(l_ref)
        acc_ref[...] = jnp.zeros_like(acc_ref)

    delta = qi * tq - ki * tk
    before = delta >= tk
    after = delta <= -tq
    sigma = jnp.where(before, slope, jnp.where(after, -slope, 0.0))

    def query_offsets():
        return lax.broadcasted_iota(jnp.int32, (tq, 1), 0).astype(F32) + delta.astype(F32)

    def scores(overlapping):
        il = query_offsets()
        jl = lax.broadcasted_iota(jnp.int32, (1, tk), 1).astype(F32)
        for m in range(2):
            q = q_ref[:, m * dh:(m + 1) * dh]
            for ci in range(tk // cw):
                cols = slice(ci * cw, (ci + 1) * cw)
                s_c = _dot_nt(q, k_ref[cols, m * dh:(m + 1) * dh])
                if overlapping:
                    s_c = s_c - slope * jnp.abs(il - jl[:, cols])
                else:
                    s_c = s_c + sigma * jl[:, cols]
                s_refs[m][:, cols] = s_c
                part_max = s_c[:, 0:LANES]
                for pt in range(1, cw // LANES):
                    part_max = jnp.maximum(part_max, s_c[:, pt * LANES:(pt + 1) * LANES])
                w_refs[m][...] = part_max if ci == 0 else jnp.maximum(w_refs[m][...], part_max)

    def softmax_and_pv():
        row_off = -sigma * query_offsets()
        for m in range(2):
            s_ref, p_ref, wide_ref = s_refs[m], p_refs[m], w_refs[m]
            m_old = m_ref[m]
            m_new = jnp.maximum(m_old, jnp.max(wide_ref[...], axis=-1, keepdims=True) + row_off)
            alpha = jnp.exp2(m_old - m_new)
            m_ref[m] = m_new
            wide_ref[...] = jnp.broadcast_to(m_new - row_off, (tq, LANES))
            per_group = ATTN_PV_ROWS // rb
            for grp in range(tq // ATTN_PV_ROWS):
                for r in range(grp * per_group, (grp + 1) * per_group):
                    rows = slice(r * rb, (r + 1) * rb)
                    shift_b = wide_ref[rows, :]
                    part_sum = None
                    for pt in range(nparts):
                        lanes = slice(pt * LANES, (pt + 1) * LANES)
                        p = jnp.exp2(s_ref[rows, lanes] - shift_b)
                        p_ref[rows, lanes] = p.astype(BF16)
                        part_sum = p if part_sum is None else part_sum + p
                    wide_ref[rows, :] = part_sum
                grows = slice(grp * ATTN_PV_ROWS, (grp + 1) * ATTN_PV_ROWS)
                acc_ref[m, grows, :] = (alpha[grows] * acc_ref[m, grows, :]
                                        + _dot(p_ref[grows, :], v_ref[...]))
            l_ref[m] = alpha * l_ref[m] + jnp.sum(wide_ref[...], axis=-1, keepdims=True)

    live = jnp.logical_not(_attn_all_zero(slopes_ref, qn_ref, kn_ref, pl.program_id(0), h, qi, ki,
                                          pl.num_programs(2), pl.num_programs(3), pl.num_programs(1),
                                          tq, tk))

    @pl.when(jnp.logical_and(live, jnp.logical_or(before, after)))
    def _():
        scores(False)

    @pl.when(jnp.logical_and(live, jnp.logical_not(jnp.logical_or(before, after))))
    def _():
        scores(True)

    @pl.when(live)
    def _():
        softmax_and_pv()

    @pl.when(ki == pl.num_programs(3) - 1)
    def _():
        lf = lam_ref[...]
        lam_full = (jnp.exp(jnp.sum(lf[0:1] * lf[1:2], axis=-1, keepdims=True))
                    - jnp.exp(jnp.sum(lf[2:3] * lf[3:4], axis=-1, keepdims=True)) + lam_init)
        o = acc_ref[0] / l_ref[0] - lam_full * (acc_ref[1] / l_ref[1])
        o_ref[...] = (_rms(o, subln_ref[...]) * (1.0 - lam_init)).astype(o_ref.dtype)


def diff_attention_core(qkv, lam, subln, lam_init):
    b, s, d3 = qkv.shape
    d = d3 // 3
    hw = 2 * A_Q_HEAD_DIM
    nh = d // hw
    tq = _pick(s, (ATTN_TILE, 512, 256))
    tk = tq
    nq = s // tq
    slopes = LOG2_E * 2.0 ** (-8.0 * jnp.arange(1, nh + 1, dtype=F32) / nh)
    norms = attn_block_norms(qkv, 2 * d, tq)
    qn = norms[:, :, :2 * nh].reshape(-1)
    kn = norms[:, :, 2 * nh:].reshape(-1)
    kern = functools.partial(_diff_attn_kernel, lam_init=lam_init, tq=tq, tk=tk)

    def key_block(bi, h, qi, ki, slopes_ref, qn_ref, kn_ref):
        dead = _attn_all_zero(slopes_ref, qn_ref, kn_ref, bi, h, qi, ki, nq, nq, nh, tq, tk)
        return jnp.where(dead, qi, ki)

    grid_spec = pltpu.PrefetchScalarGridSpec(
        num_scalar_prefetch=3,
        grid=(b, nh, nq, nq),
        in_specs=[
            pl.BlockSpec((4, A_Q_HEAD_DIM), lambda bi, h, qi, ki, *_: (0, 0)),
            pl.BlockSpec((1, hw), lambda bi, h, qi, ki, *_: (0, 0)),
            pl.BlockSpec((None, tq, hw), lambda bi, h, qi, ki, *_: (bi, qi, h)),
            pl.BlockSpec((None, tk, hw), lambda bi, h, qi, ki, *pre: (bi, key_block(bi, h, qi, ki, *pre), nh + h)),
            pl.BlockSpec((None, tk, hw),
                         lambda bi, h, qi, ki, *pre: (bi, key_block(bi, h, qi, ki, *pre), 2 * nh + h)),
        ],
        out_specs=pl.BlockSpec((None, tq, hw), lambda bi, h, qi, ki, *_: (bi, qi, h)),
        scratch_shapes=[
            pltpu.VMEM((tq, tk), F32),
            pltpu.VMEM((tq, tk), F32),
            pltpu.VMEM((tq, tk), BF16),
            pltpu.VMEM((tq, tk), BF16),
            pltpu.VMEM((tq, LANES), F32),
            pltpu.VMEM((tq, LANES), F32),
            pltpu.VMEM((2, tq, 1), F32),
            pltpu.VMEM((2, tq, 1), F32),
            pltpu.VMEM((2, tq, hw), F32),
        ],
    )
    return pl.pallas_call(
        kern,
        grid_spec=grid_spec,
        out_shape=jax.ShapeDtypeStruct((b, s, d), BF16),
        compiler_params=_cparams("parallel", "parallel", "parallel", "arbitrary"),
        name="diff_attention_core",
    )(slopes, qn, kn, lam.astype(F32), subln.reshape(1, hw).astype(F32), qkv, qkv, qkv)


POOL_HALO = 16


def _pool_kernel(xp_ref, xc_ref, xn_ref, gin_ref, gout_ref, w_ref, scale_ref, o_ref, *, seq, tm):
    i = pl.program_id(1)
    n = pl.num_programs(1)
    gin = gin_ref[...]
    xc = xc_ref[...]
    uc = _rms(xc, gin)
    up = _rms(xp_ref[...], gin) * (i > 0).astype(F32)
    un = _rms(xn_ref[...], gin) * (i < n - 1).astype(F32)
    grp = w_ref.shape[1]
    hal = POOL_HALO

    r_c = lax.broadcasted_iota(jnp.int32, (tm, tm), 0)
    c_c = lax.broadcasted_iota(jnp.int32, (tm, tm), 1)
    d_c = c_c - r_c
    r_h = lax.broadcasted_iota(jnp.int32, (tm, hal), 0)
    c_h = lax.broadcasted_iota(jnp.int32, (tm, hal), 1)
    d_p = c_h - hal - r_h
    d_n = c_h + tm - r_h
    t_abs = i * tm + lax.broadcasted_iota(jnp.int32, (tm, 1), 0)

    outs = []
    for gi, win in enumerate(POOL_WINDOWS):
        half = win // 2
        sl = slice(gi * grp, (gi + 1) * grp)
        band_c = ((d_c >= -half) & (d_c < half)).astype(F32)
        band_p = (d_p >= -half).astype(F32)
        band_n = (d_n < half).astype(F32)
        wsum = (_dot_exact_lhs(band_c, uc[:, sl]) + _dot_exact_lhs(band_p, up[:, sl])
                + _dot_exact_lhs(band_n, un[:, sl]))
        lo = jnp.clip(t_abs - half, 0, seq)
        hi = jnp.clip(t_abs + half, 0, seq)
        count = (hi - lo).astype(F32)
        pooled = wsum / count - uc[:, sl]
        outs.append(_dot(pooled.astype(BF16), w_ref[gi]))
    y = jnp.concatenate(outs, axis=-1) * scale_ref[...]
    o_ref[...] = xc + _rms(y, gout_ref[...])


def pool_sublayer(x, g_in, g_out, w_groups, scale):
    b, s, d = x.shape
    tm = _pick(s, (256, 128, 64, 32, 16))
    hal = POOL_HALO
    nh = tm // hal
    last = s // hal - 1
    kern = functools.partial(_pool_kernel, seq=s, tm=tm)
    grp = w_groups.shape[1]
    return pl.pallas_call(
        kern,
        grid=(b, s // tm),
        in_specs=[
            pl.BlockSpec((None, hal, d), lambda bi, i: (bi, jnp.maximum(i * nh - 1, 0), 0)),
            pl.BlockSpec((None, tm, d), lambda bi, i: (bi, i, 0)),
            pl.BlockSpec((None, hal, d), lambda bi, i: (bi, jnp.minimum((i + 1) * nh, last), 0)),
            pl.BlockSpec((1, d), lambda bi, i: (0, 0)),
            pl.BlockSpec((1, d), lambda bi, i: (0, 0)),
            pl.BlockSpec((len(POOL_WINDOWS), grp, grp), lambda bi, i: (0, 0, 0)),
            pl.BlockSpec((1, d), lambda bi, i: (0, 0)),
        ],
        out_specs=pl.BlockSpec((None, tm, d), lambda bi, i: (bi, i, 0)),
        out_shape=jax.ShapeDtypeStruct((b, s, d), F32),
        compiler_params=_cparams("parallel", "parallel"),
        name="pool_sublayer",
    )(x, x, x, g_in.reshape(1, d), g_out.reshape(1, d), w_groups, scale.reshape(1, d).astype(F32))


CONV_HALO = 16


def _gdn_conv_kernel(xp_ref, xc_ref, xn_ref, w_ref, o_ref, ext_ref, *, tb, n_q_blocks, n_qk_blocks):
    i = pl.program_id(1)
    j = pl.program_id(2)
    n = pl.num_programs(1)
    hal = CONV_HALO
    left = GDN_CONV // 2
    ext_ref[0:hal, :] = xp_ref[...].astype(F32) * (i > 0).astype(F32)
    ext_ref[hal:hal + tb, :] = xc_ref[...].astype(F32)
    ext_ref[hal + tb:hal + tb + hal, :] = xn_ref[...].astype(F32) * (i < n - 1).astype(F32)
    acc = None
    for kk in range(GDN_CONV):
        start = hal - left + kk
        term = ext_ref[start:start + tb, :] * w_ref[kk:kk + 1, :]
        acc = term if acc is None else acc + term
    y = _silu(acc)

    @pl.when(j >= n_qk_blocks)
    def _():
        o_ref[...] = y.astype(o_ref.dtype)

    @pl.when(j < n_qk_blocks)
    def _():
        qscale = jnp.where(j < n_q_blocks, GDN_K_DIM ** -0.5, 1.0).astype(F32)
        for hh in range(y.shape[1] // GDN_K_DIM):
            sl = slice(hh * GDN_K_DIM, (hh + 1) * GDN_K_DIM)
            t = y[:, sl]
            nrm = lax.rsqrt(jnp.sum(t * t, axis=-1, keepdims=True) + RMS_EPS) * qscale
            o_ref[:, sl] = (t * nrm).astype(o_ref.dtype)


def gdn_conv(qkvz, conv_w, qk_w, conv_dim):
    b, s, _ = qkvz.shape
    tb = _pick(s, (512, 256, 128, 64, 32, 16))
    cb = _pick(qk_w, (1024, 512, 256, 128))
    hal = CONV_HALO
    nh = tb // hal
    last = s // hal - 1
    kern = functools.partial(_gdn_conv_kernel, tb=tb, n_q_blocks=qk_w // cb, n_qk_blocks=2 * qk_w // cb)
    return pl.pallas_call(
        kern,
        grid=(b, s // tb, conv_dim // cb),
        in_specs=[
            pl.BlockSpec((None, hal, cb), lambda bi, i, j: (bi, jnp.maximum(i * nh - 1, 0), j)),
            pl.BlockSpec((None, tb, cb), lambda bi, i, j: (bi, i, j)),
            pl.BlockSpec((None, hal, cb), lambda bi, i, j: (bi, jnp.minimum((i + 1) * nh, last), j)),
            pl.BlockSpec((GDN_CONV, cb), lambda bi, i, j: (0, j)),
        ],
        out_specs=pl.BlockSpec((None, tb, cb), lambda bi, i, j: (bi, i, j)),
        out_shape=jax.ShapeDtypeStruct((b, s, conv_dim), BF16),
        scratch_shapes=[pltpu.VMEM((tb + 2 * hal, cb), F32)],
        compiler_params=_cparams("parallel", "parallel", "parallel"),
        name="gdn_conv",
    )(qkvz, qkvz, qkvz, conv_w.astype(F32))


GDN_HEAD_GROUP = 8


def _split2(x):
    hi = x.astype(BF16)
    lo = (x - hi.astype(F32)).astype(BF16)
    return hi, lo


def _dot_hi_pieces(a_pieces, b_pieces):
    ah, al = a_pieces
    bh, bl = b_pieces
    return _dot(ah, bh) + _dot(ah, bl) + _dot(al, bh)


def _gdn_core_kernel(q_ref, k_ref, v_ref, ab_ref, alog_ref, dt_ref, o_ref, s_ref, *,
                     direction, tb, n_vheads, hg):
    grp = pl.program_id(1)
    blk = pl.program_id(2)
    c = GDN_CHUNK
    nchunk = tb // c
    dk = GDN_K_DIM
    dv = GDN_V_DIM
    assert dk == 2 * c and dv == dk
    hv2 = 2 * n_vheads
    nvh = 2 * hg
    abw = ab_ref.shape[1]

    @pl.when(blk == 0)
    def _():
        s_ref[...] = jnp.zeros_like(s_ref)

    ab = ab_ref[...]
    lane = lax.broadcasted_iota(jnp.int32, ab.shape, 1)
    g_all = -jnp.exp(alog_ref[...]) * jax.nn.softplus(ab + dt_ref[...])
    beta_all = jax.nn.sigmoid(ab)

    r_t = lax.broadcasted_iota(jnp.int32, (tb, tb), 0)
    c_t = lax.broadcasted_iota(jnp.int32, (tb, tb), 1)
    same_chunk = _div_pow2(r_t, c) == _div_pow2(c_t, c)
    tri = (c_t <= r_t) if direction == 0 else (c_t >= r_t)
    cum_mat = (same_chunk & tri).astype(F32)
    gc_all = _dot_exact_lhs(cum_mat, jnp.where(lane < hv2, g_all, 0.0))

    col0 = direction * n_vheads + nvh * grp
    sel_r = lax.broadcasted_iota(jnp.int32, (abw, nvh * dv), 0)
    sel_c = lax.broadcasted_iota(jnp.int32, (abw, nvh * dv), 1)
    a_col = col0 + _div_pow2(sel_c, dv)
    gcb_all = _dot_exact_rhs(gc_all, (sel_r == a_col).astype(F32))
    bb_all = _dot(beta_all.astype(BF16), (sel_r == a_col + hv2).astype(BF16))

    gc_dup = jnp.concatenate([gc_all[(i // 2) * c:(i // 2 + 1) * c] for i in range(2 * nchunk)], axis=0)
    nrow = max(16, nvh)
    rs_r = lax.broadcasted_iota(jnp.int32, (nrow, abw), 0)
    rs_c = lax.broadcasted_iota(jnp.int32, (nrow, abw), 1)
    rsel = (rs_c == col0 + rs_r).astype(BF16)
    d_hi, d_mid, d_lo = _split3(gc_dup)
    rows_g = _dot_nt(rsel, d_hi) + _dot_nt(rsel, d_mid) + _dot_nt(rsel, d_lo)

    ii = lax.broadcasted_iota(jnp.int32, (c, 2 * c), 0)
    ll = lax.broadcasted_iota(jnp.int32, (c, 2 * c), 1)
    jj = jnp.bitwise_and(ll, c - 1)
    second = ll >= c
    if direction == 0:
        m_incl, m_strict, last_row = ii >= jj, ii > jj, c - 1
    else:
        m_incl, m_strict, last_row = ii <= jj, ii < jj, 0
    eye_p = (ii == jj).astype(F32)
    bd_r = lax.broadcasted_iota(jnp.int32, (2 * c, 2 * c), 0)
    bd_c = lax.broadcasted_iota(jnp.int32, (2 * c, 2 * c), 1)
    bd_mask = (bd_r >= c) == (bd_c >= c)
    second_row = lax.broadcasted_iota(jnp.int32, (1, 2 * c), 1) >= c
    lane_sol = lax.broadcasted_iota(jnp.int32, (c, 4 * dv), 1)
    sol_second = jnp.bitwise_and(_div_pow2(lane_sol, dv), 1) == 1
    bd2_r = lax.broadcasted_iota(jnp.int32, (2 * c, 2 * dv), 0)
    bd2_c = lax.broadcasted_iota(jnp.int32, (2 * c, 2 * dv), 1)
    bd2_mask = (bd2_r >= c) == (bd2_c >= dv)
    st_r = lax.broadcasted_iota(jnp.int32, (2 * dk, 2 * dv), 0)
    st_c = lax.broadcasted_iota(jnp.int32, (2 * dk, 2 * dv), 1)
    st_mask = (st_r >= dk) == (st_c >= dv)
    st_second = lax.broadcasted_iota(jnp.int32, (2 * dk, dv), 0) >= dk

    def block_diag(piece):
        return jnp.where(bd_mask, jnp.concatenate([piece, piece], axis=0), 0)

    order = list(range(nchunk)) if direction == 0 else list(range(nchunk - 1, -1, -1))
    probs = [(ch, hh) for ch in order for hh in range(hg)]

    pre = {}
    for ch, hh in probs:
        rows = slice(ch * c, (ch + 1) * c)
        k = k_ref[rows, hh * dk:(hh + 1) * dk]
        q = q_ref[rows, hh * dk:(hh + 1) * dk]
        k2 = jnp.concatenate([k, k], axis=0)
        gram = _dot_nt(jnp.concatenate([k, q], axis=0), k2)
        kk_p = gram[:c]
        qk_p = gram[c:]
        g_full = gcb_all[rows, 2 * hh * dv:(2 * hh + 2) * dv]
        b_full = bb_all[rows, 2 * hh * dv:(2 * hh + 2) * dv]
        g_p = jnp.where(second, g_full[:, dv:dv + 2 * c], g_full[:, :2 * c])
        b_p = jnp.where(second, b_full[:, dv:dv + 2 * c], b_full[:, :2 * c])
        r_blk = slice(ch * 2 * c, (ch + 1) * 2 * c)
        r_row = jnp.where(second_row, rows_g[2 * hh + 1:2 * hh + 2, r_blk], rows_g[2 * hh:2 * hh + 1, r_blk])
        decay = jnp.exp(jnp.where(m_incl, g_p - r_row, NEG_BIG))
        a_p = jnp.where(m_strict, b_p * kk_p * decay, 0.0)
        qkm_p = jnp.where(m_incl, qk_p * decay, 0.0)
        kf2 = jnp.concatenate([k, k], axis=1).astype(F32)
        qf2 = jnp.concatenate([q, q], axis=1).astype(F32)
        eg = jnp.exp(g_full)
        vb = v_ref[rows, hh * 2 * dv:(hh + 1) * 2 * dv].astype(F32) * b_full
        kb = kf2 * (b_full * eg)
        x = jnp.concatenate([vb, kb], axis=1)
        rhs = jnp.concatenate([jnp.where(sol_second, 0.0, x), jnp.where(sol_second, x, 0.0)],
                              axis=0).astype(BF16)
        g_last = g_full[last_row:last_row + 1, :]
        pre[(ch, hh)] = dict(mp=-a_p, t=eye_p - a_p, qkm=qkm_p.astype(BF16), rhs=rhs,
                             qs=(qf2 * eg).astype(BF16), kd=(kf2 * jnp.exp(g_last - g_full)).astype(BF16),
                             egl=jnp.exp(g_last))

    def split_and_diag(p):
        p['mpc'] = _split2(p['mp'])
        p['mbd'] = tuple(block_diag(x) for x in p['mpc'])

    for key in probs:
        split_and_diag(pre[key])
    for key in probs:
        p = pre[key]
        p['mp'] = _dot_hi_pieces(p['mpc'], p['mbd'])
    n_levels = int(math.log2(c)) - 1
    for lvl in range(n_levels):
        last = lvl == n_levels - 1
        for key in probs:
            split_and_diag(pre[key])
        for key in probs:
            p = pre[key]
            mh, ml = p['mpc']
            bh, bl = p['mbd']
            th, tl = _split2(p['t'])
            if last:
                p['t'] = p['t'] + _dot(th, bh)
            else:
                by_hi = _dot(jnp.concatenate([mh, ml, th, tl], axis=0), bh)
                by_lo = _dot(jnp.concatenate([mh, th], axis=0), bl)
                p['mp'] = by_hi[:c] + by_hi[c:2 * c] + by_lo[:c]
                p['t'] = p['t'] + (by_hi[2 * c:3 * c] + by_hi[3 * c:] + by_lo[c:])
    for key in probs:
        p = pre[key]
        sol = _dot(p['t'].astype(BF16), p['rhs'])
        p['u'] = sol[:, :2 * dv]
        p['w'] = sol[:, 2 * dv:].astype(BF16)

    for ch in order:
        rows = slice(ch * c, (ch + 1) * c)
        states = [s_ref[hh] for hh in range(hg)]
        sbs = [jnp.where(st_mask, jnp.concatenate([s, s], axis=1), 0.0).astype(BF16) for s in states]
        wq_s = [_dot(jnp.concatenate([pre[(ch, hh)]['w'], pre[(ch, hh)]['qs']], axis=0), sbs[hh])
                for hh in range(hg)]
        for hh in range(hg):
            p = pre[(ch, hh)]
            vnb = (p['u'] - wq_s[hh][:c]).astype(BF16)
            vn2 = jnp.where(bd2_mask, jnp.concatenate([vnb, vnb], axis=0), 0)
            o = wq_s[hh][c:] + _dot(p['qkm'], vn2)
            o_ref[rows, hh * 2 * dv:(hh + 1) * 2 * dv] = o.astype(o_ref.dtype)
            upd = _dot_tn(p['kd'], vnb)
            upd_d = jnp.where(st_second, upd[:, dv:], upd[:, :dv])
            egl = p['egl']
            dl = jnp.where(st_second, jnp.broadcast_to(egl[:, dv:], (2 * dk, dv)),
                           jnp.broadcast_to(egl[:, :dv], (2 * dk, dv)))
            s_ref[hh] = states[hh] * dl + upd_d


def gdn_core(qkv, ab, a_log, dt_bias, direction, n_kheads, n_vheads):
    b, s, _ = qkv.shape
    tb = _pick(s, (256, 128, 64))
    nblk = s // tb
    abw = ab.shape[-1]
    hg = _pick(n_kheads, (GDN_HEAD_GROUP, 2, 1))
    ngrp = n_kheads // hg
    pad = jnp.zeros((2 * n_vheads,), F32)
    alog_row = jnp.concatenate([a_log.reshape(-1).astype(F32), pad]).reshape(1, abw)
    dt_row = jnp.concatenate([dt_bias.reshape(-1).astype(F32), pad]).reshape(1, abw)

    def tok(blk):
        return blk if direction == 0 else nblk - 1 - blk

    kern = functools.partial(_gdn_core_kernel, direction=direction, tb=tb, n_vheads=n_vheads, hg=hg)
    qw = hg * GDN_K_DIM
    vw = hg * 2 * GDN_V_DIM
    return pl.pallas_call(
        kern,
        grid=(b, ngrp, nblk),
        in_specs=[
            pl.BlockSpec((None, tb, qw), lambda bi, g, blk: (bi, tok(blk), g)),
            pl.BlockSpec((None, tb, qw), lambda bi, g, blk: (bi, tok(blk), ngrp + g)),
            pl.BlockSpec((None, tb, vw), lambda bi, g, blk: (bi, tok(blk), ngrp + g)),
            pl.BlockSpec((None, tb, abw), lambda bi, g, blk: (bi, tok(blk), 0)),
            pl.BlockSpec((1, abw), lambda bi, g, blk: (0, 0)),
            pl.BlockSpec((1, abw), lambda bi, g, blk: (0, 0)),
        ],
        out_specs=pl.BlockSpec((None, tb, vw), lambda bi, g, blk: (bi, tok(blk), g)),
        out_shape=jax.ShapeDtypeStruct((b, s, n_vheads * GDN_V_DIM), F32),
        scratch_shapes=[pltpu.VMEM((hg, 2 * GDN_K_DIM, GDN_V_DIM), F32)],
        compiler_params=_cparams("parallel", "parallel", "arbitrary"),
        name="gdn_core_fwd" if direction == 0 else "gdn_core_bwd",
    )(qkv, qkv, qkv, ab, alog_row, dt_row)


def _gdn_gate_kernel(of_ref, ob_ref, z_ref, g_ref, o_ref):
    g = g_ref[...]
    for hh in range(of_ref.shape[1] // GDN_V_DIM):
        sl = slice(hh * GDN_V_DIM, (hh + 1) * GDN_V_DIM)
        o = of_ref[:, sl] + ob_ref[:, sl]
        z = z_ref[:, sl].astype(F32)
        o_ref[:, sl] = (_rms(o, g) * _silu(z)).astype(o_ref.dtype)


def gdn_gate(o_fwd, o_bwd, qkvz, norm_g, z_col0):
    t, w = o_fwd.shape
    tm = _pick(t, (512, 256, 128, 64, 32, 16))
    cb = _pick(w, (1024, 512, 256, 128))
    zoff = z_col0 // cb
    return pl.pallas_call(
        _gdn_gate_kernel,
        grid=(t // tm, w // cb),
        in_specs=[
            pl.BlockSpec((tm, cb), lambda i, j: (i, j)),
            pl.BlockSpec((tm, cb), lambda i, j: (i, j)),
            pl.BlockSpec((tm, cb), lambda i, j: (i, zoff + j)),
            pl.BlockSpec((1, GDN_V_DIM), lambda i, j: (0, 0)),
        ],
        out_specs=pl.BlockSpec((tm, cb), lambda i, j: (i, j)),
        out_shape=jax.ShapeDtypeStruct((t, w), BF16),
        compiler_params=_cparams("parallel", "parallel"),
        name="gdn_gate",
    )(o_fwd, o_bwd, qkvz, norm_g.reshape(1, GDN_V_DIM).astype(F32))


NA_KEY_ROWS = 4
NA_KEY_BLOCKS = 4
NA_BAND_ROWS = NA_KEY_ROWS * NA_KEY_BLOCKS
NA_BAND_LEAD = NA_WIN_ROWS // 2


def _na_kernel(bias_ref, q_ref, *refs, grid_rows):
    k_refs = refs[:NA_KEY_BLOCKS]
    v_refs = refs[NA_KEY_BLOCKS:2 * NA_KEY_BLOCKS]
    o_ref = refs[2 * NA_KEY_BLOCKS]
    i = pl.program_id(2)
    rb = NA_ROW_BLOCK
    tq = rb * GRID_W
    tkb = NA_KEY_ROWS * GRID_W
    q = q_ref[...]
    s = jnp.concatenate([_dot_nt(q, k_ref[...]) for k_ref in k_refs], axis=-1)
    q_row = i * rb + _div_pow2(lax.broadcasted_iota(jnp.int32, (tq, 1), 0), GRID_W)
    win_lo = jnp.clip(q_row - NA_WIN_ROWS // 2, 0, grid_rows - NA_WIN_ROWS)
    k_row = (i * rb - NA_BAND_LEAD
             + _div_pow2(lax.broadcasted_iota(jnp.int32, (1, NA_KEY_BLOCKS * tkb), 1), GRID_W))
    valid = (k_row >= win_lo) & (k_row < win_lo + NA_WIN_ROWS)
    s = jnp.where(valid, s + bias_ref[...], NEG_BIG)
    m = jnp.max(s, axis=-1, keepdims=True)
    p = jnp.exp(s - m)
    l = jnp.sum(p, axis=-1, keepdims=True)
    pb = p.astype(BF16)
    o = _dot(pb[:, :tkb], v_refs[0][...])
    for j in range(1, NA_KEY_BLOCKS):
        o = o + _dot(pb[:, j * tkb:(j + 1) * tkb], v_refs[j][...])
    o_ref[...] = (o / l).astype(o_ref.dtype)


def _na_dense_bias(rpb):
    h = rpb.shape[0]
    rb, w, kw, khw = NA_ROW_BLOCK, GRID_W, NA_WIN_COLS, NA_WIN_ROWS
    cols = jnp.arange(w)
    col_start = jnp.clip(cols - kw // 2, 0, w - kw)
    dc = cols[None, :] - cols[:, None]
    col_ok = (cols[None, :] >= col_start[:, None]) & (cols[None, :] < col_start[:, None] + kw)
    tiles = rpb[:, :, jnp.clip(dc + kw - 1, 0, 2 * kw - 2)]
    tiles = jnp.where(col_ok[None, None], tiles.astype(F32), NEG_BIG)
    tiles = jnp.concatenate([tiles, jnp.full((h, 1, w, w), NEG_BIG, F32)], axis=1)
    rq = jnp.arange(rb)[:, None]
    rk = jnp.arange(NA_BAND_ROWS)[None, :]
    dr = rk - NA_BAND_LEAD - rq
    idx = jnp.where(jnp.abs(dr) <= khw - 1, dr + khw - 1, 2 * khw - 1)
    dense = tiles[:, idx]
    dense = dense.transpose(0, 1, 3, 2, 4).reshape(h, rb * w, NA_BAND_ROWS * w)
    return dense


def na_core(qkv, rpb):
    b, s, d3 = qkv.shape
    d = d3 // 3
    dh = NA_HEAD_DIM
    nh = d // dh
    grid_rows = s // GRID_W
    tq = NA_ROW_BLOCK * GRID_W
    nblk = s // tq
    bias = _na_dense_bias(rpb)
    tkb = NA_KEY_ROWS * GRID_W
    n_kblk = s // tkb
    first = (NA_ROW_BLOCK // NA_KEY_ROWS, NA_BAND_LEAD // NA_KEY_ROWS)

    def band_spec(j, col0):
        return pl.BlockSpec(
            (None, tkb, dh),
            lambda bi, h, i: (bi, jnp.clip(first[0] * i - first[1] + j, 0, n_kblk - 1), col0 + h))

    kern = functools.partial(_na_kernel, grid_rows=grid_rows)
    return pl.pallas_call(
        kern,
        grid=(b, nh, nblk),
        in_specs=[
            pl.BlockSpec((None, tq, NA_BAND_ROWS * GRID_W), lambda bi, h, i: (h, 0, 0)),
            pl.BlockSpec((None, tq, dh), lambda bi, h, i: (bi, i, h)),
            *[band_spec(j, nh) for j in range(NA_KEY_BLOCKS)],
            *[band_spec(j, 2 * nh) for j in range(NA_KEY_BLOCKS)],
        ],
        out_specs=pl.BlockSpec((None, tq, dh), lambda bi, h, i: (bi, i, h)),
        out_shape=jax.ShapeDtypeStruct((b, s, d), BF16),
        compiler_params=_cparams("parallel", "parallel", "parallel"),
        name="na_core",
    )(bias, qkv, *([qkv] * (2 * NA_KEY_BLOCKS)))


def _lambda_init(layer_idx):
    return 0.8 - 0.6 * math.exp(-0.3 * layer_idx)


def _scaled_q_weight(w_qkv, scale):
    d = w_qkv.shape[1] // 3
    return jnp.concatenate([w_qkv[:, :d] * scale, w_qkv[:, d:]], axis=1).astype(BF16)


def _trunk(x, p):
    b, s, d = x.shape
    depth = p['norms'].shape[0]
    xt = x.reshape(b * s, d)
    for i in range(depth):
        g = p['norms'][i]
        xt = ffn_half_step(xt, g[0], g[1], p['ffn_w_in'][i][0], p['ffn_w_out'][i][0])
        kind, j = i % N_MIXERS, i // N_MIXERS
        if kind == 0:
            w_qkv = _scaled_q_weight(p['a_w_qkv'][j], LOG2_E * A_Q_HEAD_DIM ** -0.5)
            qkv = norm_matmul(xt, g[2], w_qkv, BF16).reshape(b, s, 3 * d)
            o = diff_attention_core(qkv, p['a_lambda'][j], p['a_subln'][j], _lambda_init(i))
            xt = matmul_norm_res(o.reshape(b * s, d), p['a_w_out'][j].astype(BF16), g[3], xt)
        elif kind == 1:
            xt = pool_sublayer(xt.reshape(b, s, d), g[2], g[3], p['pool_w'][j].astype(BF16),
                               p['pool_scale'][j]).reshape(b * s, d)
        elif kind == 2:
            n_vheads = p['gdn_a_log'].shape[-1]
            v_w = n_vheads * GDN_V_DIM
            w_in = p['gdn_w_in'][j]
            conv_dim = p['gdn_conv'].shape[-1]
            qk_w = (conv_dim - v_w) // 2
            n_kheads = qk_w // GDN_K_DIM
            qkvz = norm_matmul(xt, g[2], w_in[:, :conv_dim + v_w].astype(BF16), BF16)
            ab = norm_matmul(xt, g[2], w_in[:, conv_dim + v_w:].astype(BF16), F32)
            qkv = gdn_conv(qkvz.reshape(b, s, conv_dim + v_w), p['gdn_conv'][j], qk_w, conv_dim)
            ab3 = ab.reshape(b, s, 4 * n_vheads)
            o_f = gdn_core(qkv, ab3, p['gdn_a_log'][j], p['gdn_dt_bias'][j], 0, n_kheads, n_vheads)
            o_b = gdn_core(qkv, ab3, p['gdn_a_log'][j], p['gdn_dt_bias'][j], 1, n_kheads, n_vheads)
            gated = gdn_gate(o_f.reshape(b * s, v_w), o_b.reshape(b * s, v_w), qkvz,
                             p['gdn_norm'][j], conv_dim)
            xt = matmul_norm_res(gated, p['gdn_w_out'][j].astype(BF16), g[3], xt)
        else:
            w_qkv = _scaled_q_weight(p['na_w_qkv'][j], NA_HEAD_DIM ** -0.5)
            qkv = norm_matmul(xt, g[2], w_qkv, BF16).reshape(b, s, 3 * d)
            o = na_core(qkv, p['na_rpb'][j])
            xt = matmul_norm_res(o.reshape(b * s, d), p['na_w_out'][j].astype(BF16), g[3], xt)
        xt = ffn_half_step(xt, g[4], g[5], p['ffn_w_in'][i][1], p['ffn_w_out'][i][1])
    return xt.reshape(b, s, d)


def kernel(x_prompt, x_sample, norms, ffn_w_in, ffn_w_out, a_w_qkv, a_lambda, a_subln, a_w_out, pool_w, pool_scale, gdn_w_in, gdn_conv, gdn_a_log, gdn_dt_bias, gdn_norm, gdn_w_out, na_w_qkv, na_rpb, na_w_out):
    p = dict(norms=norms, ffn_w_in=ffn_w_in.astype(BF16), ffn_w_out=ffn_w_out.astype(BF16),
             a_w_qkv=a_w_qkv, a_lambda=a_lambda, a_subln=a_subln, a_w_out=a_w_out, pool_w=pool_w,
             pool_scale=pool_scale, gdn_w_in=gdn_w_in, gdn_conv=gdn_conv, gdn_a_log=gdn_a_log,
             gdn_dt_bias=gdn_dt_bias, gdn_norm=gdn_norm, gdn_w_out=gdn_w_out, na_w_qkv=na_w_qkv,
             na_rpb=na_rpb, na_w_out=na_w_out)
    return (_trunk(x_prompt, p), _trunk(x_sample, p))
```
